```python
import math
import jax
import jax.numpy as jnp
from jax import lax
import numpy as np

D_MODEL = 1024
BATCH = 8
SEQ = 8192
DEPTH = 2

HEAD_DIM = 64
MIX_HALF = D_MODEL // 2
GLA_HEADS = MIX_HALF // HEAD_DIM
FOX_HEADS = MIX_HALF // HEAD_DIM
GLA_RANK = 16
GLA_TAU = 16.0
GLA_CHUNK = 64
FOX_BLOCK = 128
S5_GROUP_WIDTH = 16
S5_GROUPS = MIX_HALF // S5_GROUP_WIDTH
S5_STATE = 64
S5_CHUNK = 128
SGU_GROUPS = 8
SGU_GROUP_WIDTH = MIX_HALF // SGU_GROUPS
SGU_CHUNK = 128
D_FF = 4 * D_MODEL
N_EVEN = (DEPTH + 1) // 2
N_ODD = DEPTH // 2
EPS = 1e-6
EVEN_SIZES = (MIX_HALF, MIX_HALF, MIX_HALF, MIX_HALF, GLA_RANK, MIX_HALF, MIX_HALF, MIX_HALF, FOX_HEADS)
EVEN_SPLITS = tuple(int(v) for v in np.cumsum(EVEN_SIZES)[:-1])
EVEN_WIDTH = int(sum(EVEN_SIZES))
ODD_WIDTH = 3 * MIX_HALF

kernel_name = "hybrid_gla_fox_s5_sgu_adaln_trunk"


def _rms(x):
    x = x.astype(jnp.float32)
    return x * lax.rsqrt(jnp.mean(x * x, axis=-1, keepdims=True) + EPS)


def _gla(q, k, v, log_a):
    B, S, H, Dh = q.shape
    n = S // GLA_CHUNK

    def chunks(t):
        return t.astype(jnp.float32).reshape(B, n, GLA_CHUNK, H, Dh).transpose(1, 0, 3, 2, 4)

    q = q.astype(jnp.float32) * (Dh ** -0.5)
    mask = jnp.tril(jnp.ones((GLA_CHUNK, GLA_CHUNK), dtype=bool))

    def step(state, inp):
        qc, kc, vc, lc = inp
        bc = jnp.cumsum(lc, axis=2)
        o_inter = jnp.einsum('bhtk,bhkv->bhtv', qc * jnp.exp(bc), state)
        diff = bc[:, :, :, None, :] - bc[:, :, None, :, :]
        decay = jnp.exp(jnp.where(mask[:, :, None], diff, -jnp.inf))
        scores = jnp.einsum('bhtk,bhtsk,bhsk->bhts', qc, decay, kc)
        o = o_inter + jnp.einsum('bhts,bhsv->bhtv', scores, vc)
        b_last = bc[:, :, -1:, :]
        state = (jnp.exp(b_last[:, :, 0, :])[..., None] * state
                 + jnp.einsum('bhsk,bhsv->bhkv', kc * jnp.exp(b_last - bc), vc))
        return state, o

    state0 = jnp.zeros((B, H, Dh, Dh), jnp.float32)
    _, o = lax.scan(step, state0, (chunks(q), chunks(k), chunks(v), chunks(log_a)))
    return o.transpose(1, 0, 3, 2, 4).reshape(B, S, H, Dh)


def _fox(q, k, v, f_logit, q_gain, k_gain):
    B, S, H, Dh = q.shape
    q = (_rms(q) * q_gain).transpose(0, 2, 1, 3)
    k = (_rms(k) * k_gain).transpose(0, 2, 1, 3)
    v = v.astype(jnp.float32).transpose(0, 2, 1, 3)
    cum = jnp.cumsum(jax.nn.log_sigmoid(f_logit.astype(jnp.float32)), axis=1).transpose(0, 2, 1)
    nb = S // FOX_BLOCK
    qb = q.reshape(B, H, nb, FOX_BLOCK, Dh).transpose(2, 0, 1, 3, 4)
    cb = cum.reshape(B, H, nb, FOX_BLOCK).transpose(2, 0, 1, 3)
    pos = jnp.arange(S, dtype=jnp.int32)
    pb = pos.reshape(nb, FOX_BLOCK)
    scale = Dh ** -0.5

    def block(args):
        qi, ci, pi = args
        logits = (jnp.einsum('bhqd,bhkd->bhqk', qi, k) * scale
                  + ci[..., None] - cum[:, :, None, :])
        logits = jnp.where(pi[:, None] >= pos[None, :], logits, -jnp.inf)
        return jnp.einsum('bhqk,bhkd->bhqd', jax.nn.softmax(logits, axis=-1), v)

    out = lax.map(block, (qb, cb, pb))
    return out.transpose(1, 0, 3, 2, 4).reshape(B, S, H, Dh)


def _ssm_combine(e1, e2):
    a1r, a1i, b1r, b1i = e1
    a2r, a2i, b2r, b2i = e2
    return (a2r * a1r - a2i * a1i,
            a2r * a1i + a2i * a1r,
            a2r * b1r - a2i * b1i + b2r,
            a2r * b1i + a2i * b1r + b2i)


def _s5(u, lam_re, lam_im, log_dt, b_re, b_im, c_re, c_im, d_skip):
    B, S, _ = u.shape
    f32 = jnp.float32
    lam_re, lam_im, b_re, b_im, c_re, c_im, d_skip = (
        t.astype(f32) for t in (lam_re, lam_im, b_re, b_im, c_re, c_im, d_skip))
    dt = jnp.exp(log_dt.astype(f32))[:, None]
    mag = jnp.exp(lam_re * dt)
    ang = lam_im * dt
    abar_re = mag * jnp.cos(ang)
    abar_im = mag * jnp.sin(ang)
    den = lam_re * lam_re + lam_im * lam_im
    coef_re = ((abar_re - 1.0) * lam_re + abar_im * lam_im) / den
    coef_im = (abar_im * lam_re - (abar_re - 1.0) * lam_im) / den
    bbar_re = coef_re[..., None] * b_re - coef_im[..., None] * b_im
    bbar_im = coef_re[..., None] * b_im + coef_im[..., None] * b_re
    n = S // S5_CHUNK
    uc = u.astype(f32).reshape(B, n, S5_CHUNK, S5_GROUPS, S5_GROUP_WIDTH).transpose(1, 0, 2, 3, 4)

    def step(carry, u_chunk):
        x_re0, x_im0 = carry
        bu_re = jnp.einsum('bcgi,gpi->bcgp', u_chunk, bbar_re)
        bu_im = jnp.einsum('bcgi,gpi->bcgp', u_chunk, bbar_im)
        a_re = jnp.broadcast_to(abar_re, bu_re.shape)
        a_im = jnp.broadcast_to(abar_im, bu_re.shape)
        acc_re, acc_im, x_re, x_im = lax.associative_scan(
            _ssm_combine, (a_re, a_im, bu_re, bu_im), axis=1)
        x_re = x_re + acc_re * x_re0[:, None] - acc_im * x_im0[:, None]
        x_im = x_im + acc_re * x_im0[:, None] + acc_im * x_re0[:, None]
        y = (jnp.einsum('bcgp,gip->bcgi', x_re, c_re)
             - jnp.einsum('bcgp,gip->bcgi', x_im, c_im)
             + d_skip * u_chunk)
        return (x_re[:, -1], x_im[:, -1]), y

    zeros = jnp.zeros((B, S5_GROUPS, S5_STATE), f32)
    _, y = lax.scan(step, (zeros, zeros), uc)
    return y.transpose(1, 0, 2, 3, 4).reshape(B, S, MIX_HALF)


def _sgu(z, ln_gain, ln_bias, w_s, b_s):
    B, S, _ = z.shape
    z = jax.nn.gelu(z.astype(jnp.float32))
    u, v = z[..., :MIX_HALF], z[..., MIX_HALF:]
    mu = jnp.mean(v, axis=-1, keepdims=True)
    var = jnp.mean(jnp.square(v - mu), axis=-1, keepdims=True)
    v = (v - mu) * lax.rsqrt(var + EPS) * ln_gain + ln_bias
    n = S // SGU_CHUNK
    v = v.reshape(B, n, SGU_CHUNK, SGU_GROUPS, SGU_GROUP_WIDTH)
    mask = jnp.tril(jnp.ones((SGU_CHUNK, SGU_CHUNK), dtype=bool))
    w = jnp.where(mask[None], w_s.astype(jnp.float32), 0.0)
    mixed = jnp.einsum('gts,bnsgc->bntgc', w, v) + b_s.astype(jnp.float32).T[None, None, :, :, None]
    return u * mixed.reshape(B, S, MIX_HALF)


def _even_mixer(h, w_in, w_out, w_lr, b_lr, gla_gain, b_f, q_gain, k_gain):
    B, S, _ = h.shape
    proj = jnp.einsum('bsd,de->bse', h, w_in)
    gq, gk, gv, gg, glr, fq, fk, fv, ff = jnp.split(proj, EVEN_SPLITS, axis=-1)

    def heads(t):
        return t.reshape(B, S, -1, HEAD_DIM)

    log_a = jax.nn.log_sigmoid((jnp.einsum('bsr,re->bse', glr, w_lr) + b_lr).astype(jnp.float32)) / GLA_TAU
    o_gla = _gla(heads(gq), heads(gk), heads(gv), heads(log_a))
    o_gla = _rms(o_gla) * gla_gain * jax.nn.silu(heads(gg).astype(jnp.float32))
    o_fox = _fox(heads(fq), heads(fk), heads(fv), ff + b_f, q_gain, k_gain)
    mixed = jnp.concatenate([o_gla.reshape(B, S, -1), o_fox.reshape(B, S, -1)], axis=-1)
    return jnp.einsum('bse,ed->bsd', mixed.astype(h.dtype), w_out)


def _odd_mixer(h, w_in, w_out, lam_re, lam_im, log_dt, b_re, b_im, c_re, c_im, d_skip,
               w_glu, b_glu, ln_gain, ln_bias, w_s, b_s):
    proj = jnp.einsum('bsd,de->bse', h, w_in)
    s5_in, sgu_z = proj[..., :MIX_HALF], proj[..., MIX_HALF:]
    y = jax.nn.gelu(_s5(s5_in, lam_re, lam_im, log_dt, b_re, b_im, c_re, c_im, d_skip))
    y = y * jax.nn.sigmoid(jnp.einsum('bse,ef->bsf', y, w_glu.astype(jnp.float32)) + b_glu)
    y_sgu = _sgu(sgu_z, ln_gain, ln_bias, w_s, b_s)
    mixed = jnp.concatenate([y, y_sgu], axis=-1)
    return jnp.einsum('bse,ed->bsd', mixed.astype(h.dtype), w_out)


def setup_inputs(seed: int = 0) -> dict:
    key = jax.random.key(seed)
    ks = jax.random.split(key, 30)
    f32 = jnp.float32

    def nrm(k, shape, scale):
        return jax.random.normal(k, shape, f32) * scale

    n_idx = jnp.arange(S5_STATE, dtype=f32)
    return {
        "x": nrm(ks[0], (BATCH, SEQ, D_MODEL), 1.0),
        "c": nrm(ks[1], (BATCH, D_MODEL), 1.0),
        "ada_w": nrm(ks[2], (DEPTH, D_MODEL, 6 * D_MODEL), D_MODEL ** -0.5),
        "ada_b": nrm(ks[3], (DEPTH, 6 * D_MODEL), 0.01),
        "even_w_in": nrm(ks[4], (N_EVEN, D_MODEL, EVEN_WIDTH), D_MODEL ** -0.5),
        "even_w_out": nrm(ks[5], (N_EVEN, D_MODEL, D_MODEL), D_MODEL ** -0.5),
        "gla_w_lr": nrm(ks[6], (N_EVEN, GLA_RANK, MIX_HALF), GLA_RANK ** -0.5),
        "gla_b_lr": nrm(ks[7], (N_EVEN, MIX_HALF), 0.01),
        "gla_gain": 1.0 + nrm(ks[8], (N_EVEN, GLA_HEADS, HEAD_DIM), 0.01),
        "fox_b_f": nrm(ks[9], (N_EVEN, FOX_HEADS), 0.01),
        "fox_q_gain": 1.0 + nrm(ks[10], (N_EVEN, FOX_HEADS, HEAD_DIM), 0.01),
        "fox_k_gain": 1.0 + nrm(ks[11], (N_EVEN, FOX_HEADS, HEAD_DIM), 0.01),
        "odd_w_in": nrm(ks[12], (N_ODD, D_MODEL, ODD_WIDTH), D_MODEL ** -0.5),
        "odd_w_out": nrm(ks[13], (N_ODD, D_MODEL, D_MODEL), D_MODEL ** -0.5),
        "s5_lam_re": -0.5 + nrm(ks[14], (N_ODD, S5_GROUPS, S5_STATE), 0.01),
        "s5_lam_im": jnp.pi * n_idx + nrm(ks[15], (N_ODD, S5_GROUPS, S5_STATE), 0.01),
        "s5_log_dt": jax.random.uniform(ks[16], (N_ODD, S5_GROUPS), f32,
                                        minval=math.log(1e-3), maxval=math.log(1e-1)),
        "s5_b_re": nrm(ks[17], (N_ODD, S5_GROUPS, S5_STATE, S5_GROUP_WIDTH), (2 * S5_GROUP_WIDTH) ** -0.5),
        "s5_b_im": nrm(ks[18], (N_ODD, S5_GROUPS, S5_STATE, S5_GROUP_WIDTH), (2 * S5_GROUP_WIDTH) ** -0.5),
        "s5_c_re": nrm(ks[19], (N_ODD, S5_GROUPS, S5_GROUP_WIDTH, S5_STATE), (2 * S5_STATE) ** -0.5),
        "s5_c_im": nrm(ks[20], (N_ODD, S5_GROUPS, S5_GROUP_WIDTH, S5_STATE), (2 * S5_STATE) ** -0.5),
        "s5_d": nrm(ks[21], (N_ODD, S5_GROUPS, S5_GROUP_WIDTH), 1.0),
        "s5_w_glu": nrm(ks[22], (N_ODD, MIX_HALF, MIX_HALF), MIX_HALF ** -0.5),
        "s5_b_glu": nrm(ks[23], (N_ODD, MIX_HALF), 0.01),
        "sgu_ln_gain": 1.0 + nrm(ks[24], (N_ODD, MIX_HALF), 0.01),
        "sgu_ln_bias": nrm(ks[25], (N_ODD, MIX_HALF), 0.01),
        "sgu_w_s": nrm(ks[26], (N_ODD, SGU_GROUPS, SGU_CHUNK, SGU_CHUNK), SGU_CHUNK ** -0.5),
        "sgu_b_s": 1.0 + nrm(ks[27], (N_ODD, SGU_GROUPS, SGU_CHUNK), 0.01),
        "mlp_w1": nrm(ks[28], (DEPTH, D_MODEL, D_FF), D_MODEL ** -0.5),
        "mlp_w2": nrm(ks[29], (DEPTH, D_FF, D_MODEL), D_FF ** -0.5),
    }


def reference(x, c, ada_w, ada_b, even_w_in, even_w_out, gla_w_lr, gla_b_lr, gla_gain,
              fox_b_f, fox_q_gain, fox_k_gain, odd_w_in, odd_w_out, s5_lam_re, s5_lam_im,
              s5_log_dt, s5_b_re, s5_b_im, s5_c_re, s5_c_im, s5_d, s5_w_glu, s5_b_glu,
              sgu_ln_gain, sgu_ln_bias, sgu_w_s, sgu_b_s, mlp_w1, mlp_w2):
    c_act = jax.nn.silu(c)
    for layer in range(DEPTH):
        mod = jnp.einsum('bd,de->be', c_act, ada_w[layer]) + ada_b[layer]
        sh1, sc1, g1, sh2, sc2, g2 = jnp.split(mod.astype(jnp.float32), 6, axis=-1)
        h = (_rms(x) * (1.0 + sc1[:, None]) + sh1[:, None]).astype(x.dtype)
        i = layer // 2
        if layer % 2 == 0:
            y = _even_mixer(h, even_w_in[i], even_w_out[i], gla_w_lr[i], gla_b_lr[i], gla_gain[i],
                            fox_b_f[i], fox_q_gain[i], fox_k_gain[i])
        else:
            y = _odd_mixer(h, odd_w_in[i], odd_w_out[i], s5_lam_re[i], s5_lam_im[i], s5_log_dt[i],
                           s5_b_re[i], s5_b_im[i], s5_c_re[i], s5_c_im[i], s5_d[i], s5_w_glu[i],
                           s5_b_glu[i], sgu_ln_gain[i], sgu_ln_bias[i], sgu_w_s[i], sgu_b_s[i])
        x = x + (g1[:, None] * y).astype(x.dtype)
        h = (_rms(x) * (1.0 + sc2[:, None]) + sh2[:, None]).astype(x.dtype)
        hid = jnp.square(jax.nn.relu(jnp.einsum('bsd,df->bsf', h, mlp_w1[layer])))
        x = x + (g2[:, None] * jnp.einsum('bsf,fd->bsd', hid, mlp_w2[layer])).astype(x.dtype)
    return x
```

```python
import functools

import numpy as np
import jax
import jax.numpy as jnp
from jax import lax
from jax.experimental import pallas as pl
from jax.experimental.pallas import tpu as pltpu

F32 = jnp.float32
BF16 = jnp.bfloat16

EPS = 1e-6
HEAD_DIM = 64
LANES = 128
GLA_RANK = 16
GLA_TAU = 16.0
GLA_CHUNK = 64
S5_GROUP_WIDTH = 16
S5_STATE = 64
S5_CHUNK = 128
SGU_GROUPS = 8
SGU_CHUNK = 128
VMEM_LIMIT_BYTES = 56 * 1024 * 1024
NEG_BIG = -1e30


def _cparams(*sem):
    return pltpu.CompilerParams(dimension_semantics=sem, vmem_limit_bytes=VMEM_LIMIT_BYTES)


def _const_spec(shape):
    return pl.BlockSpec(shape, lambda *_: (0,) * len(shape), pipeline_mode=pl.Buffered(1))


def _dot(a, b):
    return jnp.dot(a, b, preferred_element_type=F32)


def _dot_nt(a, b):
    return lax.dot_general(a, b, (((1,), (1,)), ((), ())), preferred_element_type=F32)


def _dot_tn(a, b):
    return lax.dot_general(a, b, (((0,), (0,)), ((), ())), preferred_element_type=F32)


def _split2(x):
    hi = x.astype(BF16)
    lo = (x - hi.astype(F32)).astype(BF16)
    return hi, lo


def _split3(x):
    hi = x.astype(BF16)
    r = x - hi.astype(F32)
    mid = r.astype(BF16)
    lo = (r - mid.astype(F32)).astype(BF16)
    return hi, mid, lo


def _log_sigmoid(z):
    return jnp.minimum(z, 0.0) - jnp.log(1.0 + jnp.exp(-jnp.abs(z)))


def _norm_mod(x, sc, sh):
    ms = jnp.mean(x * x, axis=-1, keepdims=True)
    return x * lax.rsqrt(ms + EPS) * (1.0 + sc) + sh


def _ada_kernel(c_ref, w_ref, b_ref, o_ref):
    c = c_ref[...]
    ca = c * jax.nn.sigmoid(c)
    o_ref[0] = _dot(ca, w_ref[0]) + b_ref[0]


def _ada_mod(c, ada_w, ada_b):
    depth, d, n = ada_w.shape
    bsz = c.shape[0]
    tn = n // 4
    return pl.pallas_call(
        _ada_kernel,
        grid=(depth, n // tn),
        in_specs=[pl.BlockSpec((bsz, d), lambda l, j: (0, 0)),
                  pl.BlockSpec((1, d, tn), lambda l, j: (l, 0, j)),
                  pl.BlockSpec((1, 1, tn), lambda l, j: (l, 0, j))],
        out_specs=pl.BlockSpec((1, bsz, tn), lambda l, j: (l, 0, j)),
        out_shape=jax.ShapeDtypeStruct((depth, bsz, n), F32),
        compiler_params=_cparams("parallel", "parallel"),
        name="ada_mod",
    )(c, ada_w, ada_b.reshape(depth, 1, n))


def _inproj_kernel(x_ref, sc_ref, sh_ref, wm_ref, *rest, n_chunk, with_small):
    if with_small:
        ws_ref, om_ref, os_ref = rest
    else:
        (om_ref,) = rest
    h = _norm_mod(x_ref[0], sc_ref[0], sh_ref[0]).astype(BF16)
    n = wm_ref.shape[1]
    for c0 in range(0, n, n_chunk):
        om_ref[0, :, c0:c0 + n_chunk] = _dot(h, wm_ref[:, c0:c0 + n_chunk]).astype(BF16)
    if with_small:
        os_ref[0] = _dot(h, ws_ref[...])


def _inproj(x, sc, sh, w_main, w_small, tm):
    bsz, s, d = x.shape
    n = w_main.shape[1]
    with_small = w_small is not None
    in_specs = [pl.BlockSpec((1, tm, d), lambda b, i: (b, i, 0)),
                pl.BlockSpec((1, 1, d), lambda b, i: (b, 0, 0)),
                pl.BlockSpec((1, 1, d), lambda b, i: (b, 0, 0)),
                _const_spec((d, n))]
    args = [x, sc, sh, w_main]
    out_specs = [pl.BlockSpec((1, tm, n), lambda b, i: (b, i, 0))]
    out_shape = [jax.ShapeDtypeStruct((bsz, s, n), BF16)]
    if with_small:
        in_specs.append(_const_spec((d, LANES)))
        args.append(w_small)
        out_specs.append(pl.BlockSpec((1, tm, LANES), lambda b, i: (b, i, 0)))
        out_shape.append(jax.ShapeDtypeStruct((bsz, s, LANES), F32))
    return pl.pallas_call(
        functools.partial(_inproj_kernel, n_chunk=512, with_small=with_small),
        grid=(bsz, s // tm),
        in_specs=in_specs, out_specs=out_specs, out_shape=out_shape,
        compiler_params=_cparams("parallel", "parallel"),
        name="inproj",
    )(*args)


def _gla_kernel(q_ref, k_ref, v_ref, g_ref, sm_ref, wlr_ref, blr_ref, gain_ref, o_ref,
                state_ref, *, n_heads, n_chunks):
    C = GLA_CHUNK
    mid = C // 2 - 1

    @pl.when(pl.program_id(1) == 0)
    def _():
        state_ref[...] = jnp.zeros_like(state_ref)

    row = lax.broadcasted_iota(jnp.int32, (C, C), 0)
    col = lax.broadcasted_iota(jnp.int32, (C, C), 1)
    causal = col <= row
    tril = causal.astype(BF16)
    wlr = wlr_ref[...]
    blr = blr_ref[...]
    gain = gain_ref[...]

    def chunk(c, carry):
        r0 = pl.multiple_of(c * C, C)
        rows = pl.ds(r0, C)
        z = _dot(sm_ref[0, rows, :], wlr) + blr
        lc = _log_sigmoid(z) * (1.0 / GLA_TAU)
        hi, lo = _split2(lc)
        bc = _dot(tril, hi) + _dot(tril, lo)
        m = bc[mid:mid + 1, :]
        bl = bc[C - 1:C, :]
        qt = q_ref[0, rows, :].astype(F32) * (HEAD_DIM ** -0.5) * jnp.exp(bc - m)
        kt = k_ref[0, rows, :].astype(F32) * jnp.exp(m - bc)
        qin = (qt * jnp.exp(m)).astype(BF16)
        kst = (kt * jnp.exp(bl - m)).astype(BF16)
        qt = qt.astype(BF16)
        kt = kt.astype(BF16)
        ebl = jnp.exp(bl)
        v = v_ref[0, rows, :]
        outs = []
        for h in range(n_heads):
            sl = slice(h * HEAD_DIM, (h + 1) * HEAD_DIM)
            a = jnp.where(causal, _dot_nt(qt[:, sl], kt[:, sl]), 0.0)
            st = state_ref[h]
            o = _dot(a.astype(BF16), v[:, sl]) + _dot_nt(qin[:, sl], st.astype(BF16))
            state_ref[h] = st * ebl[:, sl] + _dot_tn(v[:, sl], kst[:, sl])
            ms = jnp.mean(o * o, axis=-1, keepdims=True)
            outs.append(o * lax.rsqrt(ms + EPS))
        o_all = jnp.concatenate(outs, axis=-1)
        g = g_ref[0, rows, :].astype(F32)
        o_ref[0, rows, :] = (o_all * gain * (g * jax.nn.sigmoid(g))).astype(BF16)
        return carry

    lax.fori_loop(0, n_chunks, chunk, 0)


def _gla(proj, small, wlr_pad, blr, gain, tt):
    bsz, s, _ = proj.shape
    w = wlr_pad.shape[1]
    n_heads = w // HEAD_DIM

    def col(j):
        return pl.BlockSpec((1, tt, w), lambda b, i, j=j: (b, i, j))

    return pl.pallas_call(
        functools.partial(_gla_kernel, n_heads=n_heads, n_chunks=tt // GLA_CHUNK),
        grid=(bsz, s // tt),
        in_specs=[col(0), col(1), col(2), col(3),
                  pl.BlockSpec((1, tt, LANES), lambda b, i: (b, i, 0)),
                  _const_spec((LANES, w)), _const_spec((1, w)), _const_spec((1, w))],
        out_specs=pl.BlockSpec((1, tt, w), lambda b, i: (b, i, 0)),
        out_shape=jax.ShapeDtypeStruct((bsz, s, w), BF16),
        scratch_shapes=[pltpu.VMEM((n_heads, HEAD_DIM, HEAD_DIM), F32)],
        compiler_params=_cparams("parallel", "arbitrary"),
        name="gla",
    )(proj, proj, proj, proj, small, wlr_pad, blr, gain)


FF_LANE0 = GLA_RANK


def _fox_prep_kernel(q_ref, k_ref, sm_ref, bf_ref, qg_ref, kg_ref, tril_ref, bd_ref,
                     sq_ref, sk_ref, cq_ref, ck_ref, qo_ref, ko_ref, carry_ref, *, n_heads):
    @pl.when(pl.program_id(1) == 0)
    def _():
        carry_ref[...] = jnp.zeros_like(carry_ref)

    sm = sm_ref[0]
    tt = sm.shape[0]
    lane = lax.broadcasted_iota(jnp.int32, (tt, LANES), 1)
    is_ff = (lane >= FF_LANE0) & (lane < FF_LANE0 + n_heads)
    ls = jnp.where(is_ff, _log_sigmoid(sm + bf_ref[...]), 0.0)
    tril = tril_ref[...]
    p0, p1, p2 = _split3(ls)
    cum = carry_ref[0:1, :] + (_dot(tril, p0) + _dot(tril, p1) + _dot(tril, p2))
    carry_ref[0:1, :] = cum[tt - 1:tt, :]
    c0, c1, c2 = _split3(cum)
    ex_q = _dot(c0, sq_ref[0]) + _dot(c1, sq_ref[1]) + _dot(c2, sq_ref[2]) + cq_ref[...]
    ex_k = _dot(c0, sk_ref[0]) + _dot(c1, sk_ref[1]) + _dot(c2, sk_ref[2]) + ck_ref[...]
    bd = bd_ref[...]
    low = lane < HEAD_DIM

    def normed(x_ref, gain_ref, scale, pair):
        x = x_ref[0, :, pair * LANES:(pair + 1) * LANES].astype(F32)
        hi, lo = _split2(x * x)
        ms = _dot(hi, bd) + _dot(lo, bd)
        return x * lax.rsqrt(ms + EPS) * (gain_ref[:, pair * LANES:(pair + 1) * LANES] * scale)

    for pair in range(n_heads // 2):
        qn = normed(q_ref, qg_ref, HEAD_DIM ** -0.5, pair)
        kn = normed(k_ref, kg_ref, 1.0, pair)
        for odd in range(2):
            h = 2 * pair + odd
            if odd:
                qh = pltpu.roll(qn, HEAD_DIM, axis=1)
                kh = pltpu.roll(kn, HEAD_DIM, axis=1)
            else:
                qh, kh = qn, kn
            qo_ref[0, h] = jnp.where(low, qh, ex_q[:, h * LANES:(h + 1) * LANES]).astype(BF16)
            ko_ref[0, h] = jnp.where(low, kh, ex_k[:, h * LANES:(h + 1) * LANES]).astype(BF16)


def _fox_prep_consts(n_heads):
    sq = np.zeros((3, LANES, n_heads * LANES), np.float32)
    sk = np.zeros((3, LANES, n_heads * LANES), np.float32)
    cq = np.zeros((1, n_heads * LANES), np.float32)
    ck = np.zeros((1, n_heads * LANES), np.float32)
    for h in range(n_heads):
        base = h * LANES + HEAD_DIM
        for r in range(3):
            sq[r, FF_LANE0 + h, base + r] = 1.0
            sk[r, FF_LANE0 + h, base + 3 + r] = -1.0
            cq[0, base + 3 + r] = 1.0
            ck[0, base + r] = 1.0
    bd = np.kron(np.eye(LANES // HEAD_DIM, dtype=np.float32),
                 np.full((HEAD_DIM, HEAD_DIM), 1.0 / HEAD_DIM, np.float32))
    return (jnp.asarray(sq, BF16), jnp.asarray(sk, BF16), jnp.asarray(cq), jnp.asarray(ck),
            jnp.asarray(bd, BF16))


def _fox_prep(proj, small, bf_pad, q_gain, k_gain, q_col, k_col, tt):
    bsz, s, _ = proj.shape
    w = q_gain.shape[1]
    n_heads = w // HEAD_DIM
    sq, sk, cq, ck, bd = _fox_prep_consts(n_heads)
    tril = jnp.asarray(np.tril(np.ones((tt, tt), np.float32)), BF16)
    out_spec = pl.BlockSpec((1, n_heads, tt, LANES), lambda b, i: (b, 0, i, 0))
    out_sds = jax.ShapeDtypeStruct((bsz, n_heads, s, LANES), BF16)
    return pl.pallas_call(
        functools.partial(_fox_prep_kernel, n_heads=n_heads),
        grid=(bsz, s // tt),
        in_specs=[pl.BlockSpec((1, tt, w), lambda b, i: (b, i, q_col)),
                  pl.BlockSpec((1, tt, w), lambda b, i: (b, i, k_col)),
                  pl.BlockSpec((1, tt, LANES), lambda b, i: (b, i, 0)),
                  _const_spec((1, LANES)), _const_spec((1, w)), _const_spec((1, w)),
                  _const_spec((tt, tt)), _const_spec((LANES, LANES)),
                  _const_spec(sq.shape), _const_spec(sk.shape),
                  _const_spec(cq.shape), _const_spec(ck.shape)],
        out_specs=[out_spec, out_spec],
        out_shape=[out_sds, out_sds],
        scratch_shapes=[pltpu.VMEM((8, LANES), F32)],
        compiler_params=_cparams("parallel", "arbitrary"),
        name="fox_prep",
    )(proj, proj, small, bf_pad, q_gain, k_gain, tril, bd, sq, sk, cq, ck)


V_ROWS = HEAD_DIM + 8


def _fox_attn_kernel(q_ref, k_ref, vt_ref, o_ref, *, tq, tk):
    i = pl.program_id(2)
    key = lax.broadcasted_iota(jnp.int32, (tk, tq), 0)
    qry = lax.broadcasted_iota(jnp.int32, (tk, tq), 1)
    diag_ok = key <= qry
    outs = []
    for hh in range(2):
        q = q_ref[0, hh]

        def update(j, carry, masked):
            m, acc = carry
            kb = k_ref[0, hh, pl.ds(pl.multiple_of(j * tk, tk), tk), :]
            s = _dot_nt(kb, q)
            if masked:
                s = jnp.where(diag_ok, s, NEG_BIG)
            m_new = jnp.maximum(m, jnp.max(s, axis=0, keepdims=True))
            p = jnp.exp(s - m_new).astype(BF16)
            acc = jnp.exp(m - m_new) * acc + _dot(vt_ref[0, hh, j], p)
            return m_new, acc

        init = (jnp.full((1, tq), NEG_BIG, F32), jnp.zeros((V_ROWS, tq), F32))
        carry = lax.fori_loop(0, i, functools.partial(update, masked=False), init)
        _, acc = update(i, carry, True)
        outs.append(acc[:HEAD_DIM] / acc[HEAD_DIM:HEAD_DIM + 1])
    o_ref[0] = jnp.concatenate(outs, axis=0).T.astype(BF16)


def _fox_attn(qp, kp, vt, tq):
    bsz, n_heads, s, _ = qp.shape
    tk = vt.shape[-1]
    nkb = s // tk
    return pl.pallas_call(
        functools.partial(_fox_attn_kernel, tq=tq, tk=tk),
        grid=(bsz, n_heads // 2, s // tq),
        in_specs=[pl.BlockSpec((1, 2, tq, LANES), lambda b, p, i: (b, p, i, 0)),
                  pl.BlockSpec((1, 2, s, LANES), lambda b, p, i: (b, p, 0, 0)),
                  pl.BlockSpec((1, 2, nkb, V_ROWS, tk), lambda b, p, i: (b, p, 0, 0, 0))],
        out_specs=pl.BlockSpec((1, tq, LANES), lambda b, p, i: (b, i, p)),
        out_shape=jax.ShapeDtypeStruct((bsz, s, n_heads * HEAD_DIM), BF16),
        compiler_params=_cparams("parallel", "parallel", "arbitrary"),
        name="fox_attn",
    )(qp, kp, vt)


def _s5_kernel(u_ref, kt_ref, min_ref, mout_ref, a_ref, d_ref, o_ref,
               t_ref, v_ref, xp_ref, *, bsz, n_col):
    L = S5_CHUNK
    W = S5_GROUP_WIDTH
    nc = u_ref.shape[1]
    row = lax.broadcasted_iota(jnp.int32, (L, L), 0)
    colm = lax.broadcasted_iota(jnp.int32, (L, L), 1)
    causal = colm >= row

    def build(j, carry):
        kj = kt_ref[0, j]
        for i in range(W):
            tile = jnp.broadcast_to(kj[i:i + 1, :], (L, L))
            tile = pltpu.roll(tile, 0, axis=1, stride=1, stride_axis=0)
            t_ref[pl.ds(pl.multiple_of(j * L, L), L), i * L:(i + 1) * L] = (
                jnp.where(causal, tile, 0.0).astype(BF16))
        return carry

    lax.fori_loop(0, W, build, 0)

    u = u_ref[0]
    v_ref[...] = _dot(u, min_ref[0])
    a1 = a_ref[0, 0:1, :]
    a2 = a_ref[0, 1:2, :]

    def scan(c, x):
        rows = pl.ds(pl.multiple_of(c * bsz, bsz), bsz)
        xp_ref[rows, :] = x
        return x * a1 + pltpu.roll(x, S5_STATE, axis=1) * a2 + v_ref[rows, :]

    lax.fori_loop(0, nc // bsz, scan, jnp.zeros((bsz, 2 * S5_STATE), F32))

    xp = xp_ref[...].astype(BF16)
    for c0 in range(0, W * L, n_col):
        cs = slice(c0, c0 + n_col)
        y = (_dot(u, t_ref[:, cs]) + _dot(xp, mout_ref[0, :, cs])
             + d_ref[0, :, cs] * u[:, cs].astype(F32))
        o_ref[0, :, cs] = jax.nn.gelu(y).astype(BF16)


def _s5(u_g, ktab, m_in, m_out, a_pack, d_vec, bsz):
    groups, nc, wl = u_g.shape
    return pl.pallas_call(
        functools.partial(_s5_kernel, bsz=bsz, n_col=512),
        grid=(groups,),
        in_specs=[pl.BlockSpec((1, nc, wl), lambda g: (g, 0, 0)),
                  pl.BlockSpec((1, S5_GROUP_WIDTH, S5_GROUP_WIDTH, S5_CHUNK), lambda g: (g, 0, 0, 0)),
                  pl.BlockSpec((1, wl, 2 * S5_STATE), lambda g: (g, 0, 0)),
                  pl.BlockSpec((1, 2 * S5_STATE, wl), lambda g: (g, 0, 0)),
                  pl.BlockSpec((1, 2, 2 * S5_STATE), lambda g: (g, 0, 0)),
                  pl.BlockSpec((1, 1, wl), lambda g: (g, 0, 0))],
        out_specs=pl.BlockSpec((1, nc, wl), lambda g: (g, 0, 0)),
        out_shape=jax.ShapeDtypeStruct((groups, nc, wl), BF16),
        scratch_shapes=[pltpu.VMEM((wl, wl), BF16),
                        pltpu.VMEM((nc, 2 * S5_STATE), F32),
                        pltpu.VMEM((nc, 2 * S5_STATE), F32)],
        compiler_params=_cparams("parallel"),
        name="s5",
    )(u_g, ktab, m_in, m_out, a_pack, d_vec)


def _s5_tables(lam_re, lam_im, log_dt, b_re, b_im, c_re, c_im, d_skip):
    L = S5_CHUNK
    dt = jnp.exp(log_dt)[:, None]
    mag = jnp.exp(lam_re * dt)
    ang = lam_im * dt
    ar, ai = mag * jnp.cos(ang), mag * jnp.sin(ang)
    den = lam_re * lam_re + lam_im * lam_im
    cr = ((ar - 1.0) * lam_re + ai * lam_im) / den
    ci = (ai * lam_re - (ar - 1.0) * lam_im) / den
    bbr = cr[..., None] * b_re - ci[..., None] * b_im
    bbi = cr[..., None] * b_im + ci[..., None] * b_re
    pr, pi = jnp.ones_like(ar)[None], jnp.zeros_like(ar)[None]
    sr, si = ar, ai
    n = 1
    while n <= L:
        pr, pi = (jnp.concatenate([pr, pr * sr - pi * si], 0),
                  jnp.concatenate([pi, pr * si + pi * sr], 0))
        sr, si = sr * sr - si * si, 2.0 * sr * si
        n *= 2
    pr, pi = pr[:L + 1], pi[:L + 1]
    hp = lax.Precision.HIGHEST
    er = c_re[None] * pr[:L, :, None, :] - c_im[None] * pi[:L, :, None, :]
    ei = -(c_re[None] * pi[:L, :, None, :] + c_im[None] * pr[:L, :, None, :])
    ktab = (jnp.einsum('lgip,gpj->gjil', er, bbr, precision=hp)
            + jnp.einsum('lgip,gpj->gjil', ei, bbi, precision=hp))
    qr, qi = pr[:L][::-1], pi[:L][::-1]
    min_re = qr[:, :, :, None] * bbr[None] - qi[:, :, :, None] * bbi[None]
    min_im = qr[:, :, :, None] * bbi[None] + qi[:, :, :, None] * bbr[None]
    m_in = jnp.concatenate([min_re, min_im], axis=2)
    m_in = m_in.transpose(1, 3, 0, 2).reshape(m_in.shape[1], -1, 2 * S5_STATE)
    tr, ti = pr[1:L + 1], pi[1:L + 1]
    mo_re = c_re[None] * tr[:, :, None, :] - c_im[None] * ti[:, :, None, :]
    mo_im = -(c_re[None] * ti[:, :, None, :] + c_im[None] * tr[:, :, None, :])
    m_out = jnp.concatenate([mo_re, mo_im], axis=3)
    m_out = m_out.transpose(1, 3, 2, 0).reshape(m_out.shape[1], 2 * S5_STATE, -1)
    a_pack = jnp.stack([jnp.concatenate([pr[L], pr[L]], -1),
                        jnp.concatenate([-pi[L], pi[L]], -1)], axis=1)
    d_vec = jnp.repeat(d_skip, L, axis=-1)[:, None, :]
    return ktab, m_in.astype(BF16), m_out.astype(BF16), a_pack, d_vec


def _sgu_kernel(u_ref, v_ref, w_ref, bias_ref, lg_ref, lb_ref, o_ref, *, n_chunks):
    L = SGU_CHUNK
    row = lax.broadcasted_iota(jnp.int32, (L, L), 0)
    colm = lax.broadcasted_iota(jnp.int32, (L, L), 1)
    causal = colm <= row
    v = jax.nn.gelu(v_ref[0].astype(F32))
    mu = jnp.mean(v, axis=-1, keepdims=True)
    vc = v - mu
    var = jnp.mean(vc * vc, axis=-1, keepdims=True)
    vn = (vc * lax.rsqrt(var + EPS) * lg_ref[...] + lb_ref[...]).astype(BF16)
    gw = vn.shape[1] // SGU_GROUPS
    ws = [jnp.where(causal, w_ref[g], 0.0).astype(BF16) for g in range(SGU_GROUPS)]
    for c in range(n_chunks):
        rows = slice(c * L, (c + 1) * L)
        mixed = jnp.concatenate(
            [_dot(ws[g], vn[rows, g * gw:(g + 1) * gw]) for g in range(SGU_GROUPS)], axis=-1)
        u = jax.nn.gelu(u_ref[0, rows, :].astype(F32))
        o_ref[0, rows, :] = (u * (mixed + bias_ref[...])).astype(BF16)


def _sgu(proj, w_s, bias_full, ln_gain, ln_bias, u_col, v_col, tt):
    bsz, s, _ = proj.shape
    w = ln_gain.shape[1]
    return pl.pallas_call(
        functools.partial(_sgu_kernel, n_chunks=tt // SGU_CHUNK),
        grid=(bsz, s // tt),
        in_specs=[pl.BlockSpec((1, tt, w), lambda b, i: (b, i, u_col)),
                  pl.BlockSpec((1, tt, w), lambda b, i: (b, i, v_col)),
                  _const_spec(w_s.shape), _const_spec(bias_full.shape),
                  _const_spec((1, w)), _const_spec((1, w))],
        out_specs=pl.BlockSpec((1, tt, w), lambda b, i: (b, i, 0)),
        out_shape=jax.ShapeDtypeStruct((bsz, s, w), BF16),
        compiler_params=_cparams("parallel", "parallel"),
        name="sgu",
    )(proj, proj, w_s, bias_full, ln_gain, ln_bias)


def _tail_kernel(x_ref, a_ref, b_ref, wo_ref, g1_ref, sc_ref, sh_ref, g2_ref, w1_ref, w2_ref,
                 *rest, f_chunk, with_glu):
    if with_glu:
        wg_ref, bg_ref, o_ref = rest
    else:
        (o_ref,) = rest
    a = a_ref[0]
    half = a.shape[1]
    if with_glu:
        af = a.astype(F32)
        a = (af * jax.nn.sigmoid(_dot(a, wg_ref[...]) + bg_ref[...])).astype(BF16)
    y = _dot(a, wo_ref[:half, :]) + _dot(b_ref[0], wo_ref[half:, :])
    x1 = x_ref[0] + g1_ref[0] * y
    h = _norm_mod(x1, sc_ref[0], sh_ref[0]).astype(BF16)
    f = w1_ref.shape[1]
    acc = None
    for f0 in range(0, f, f_chunk):
        hid = jnp.maximum(_dot(h, w1_ref[:, f0:f0 + f_chunk]), 0.0)
        part = _dot((hid * hid).astype(BF16), w2_ref[f0:f0 + f_chunk, :])
        acc = part if acc is None else acc + part
    o_ref[0] = x1 + g2_ref[0] * acc


def _tail(x, mix_a, mix_b, w_out, g1, sc, sh, g2, w1, w2, glu, tm):
    bsz, s, d = x.shape
    half = mix_a.shape[-1]
    f = w1.shape[1]
    row = pl.BlockSpec((1, 1, d), lambda b, i: (b, 0, 0))
    in_specs = [pl.BlockSpec((1, tm, d), lambda b, i: (b, i, 0)),
                pl.BlockSpec((1, tm, half), lambda b, i: (b, i, 0)),
                pl.BlockSpec((1, tm, half), lambda b, i: (b, i, 0)),
                _const_spec((d, d)), row, row, row, row,
                _const_spec((d, f)), _const_spec((f, d))]
    args = [x, mix_a, mix_b, w_out, g1, sc, sh, g2, w1, w2]
    if glu is not None:
        in_specs += [_const_spec((half, half)), _const_spec((1, half))]
        args += list(glu)
    return pl.pallas_call(
        functools.partial(_tail_kernel, f_chunk=1024, with_glu=glu is not None),
        grid=(bsz, s // tm),
        in_specs=in_specs,
        out_specs=pl.BlockSpec((1, tm, d), lambda b, i: (b, i, 0)),
        out_shape=jax.ShapeDtypeStruct((bsz, s, d), F32),
        compiler_params=_cparams("parallel", "parallel"),
        name="tail",
    )(*args)


def _tile(s, want):
    t = min(s, want)
    assert s % t == 0, (s, t)
    return t


def kernel(x, c, ada_w, ada_b, even_w_in, even_w_out, gla_w_lr, gla_b_lr, gla_gain, fox_b_f, fox_q_gain, fox_k_gain, odd_w_in, odd_w_out, s5_lam_re, s5_lam_im, s5_log_dt, s5_b_re, s5_b_im, s5_c_re, s5_c_im, s5_d, s5_w_glu, s5_b_glu, sgu_ln_gain, sgu_ln_bias, sgu_w_s, sgu_b_s, mlp_w1, mlp_w2):
    bsz, s, d = x.shape
    half = d // 2
    n_heads = half // HEAD_DIM
    tm = _tile(s, 512)

    mod = _ada_mod(c, ada_w, ada_b)

    def mods(layer):
        return [m[:, None, :] for m in jnp.split(mod[layer], 6, axis=-1)]

    sh1, sc1, g1, sh2, sc2, g2 = mods(0)
    w_in = even_w_in[0]
    o_lr = 4 * half
    o_fq = o_lr + GLA_RANK
    o_ff = o_fq + 3 * half
    w_main = jnp.concatenate([w_in[:, :o_lr], w_in[:, o_fq:o_ff]], axis=1).astype(BF16)
    w_small = jnp.concatenate(
        [w_in[:, o_lr:o_fq], w_in[:, o_ff:],
         jnp.zeros((d, LANES - GLA_RANK - n_heads), w_in.dtype)], axis=1).astype(BF16)
    proj, small = _inproj(x, sc1, sh1, w_main, w_small, tm)

    wlr_pad = jnp.concatenate([gla_w_lr[0], jnp.zeros((LANES - GLA_RANK, half), F32)], axis=0)
    o_gla = _gla(proj, small, wlr_pad, gla_b_lr[0][None, :], gla_gain[0].reshape(1, half), tm)

    bf_pad = jnp.zeros((1, LANES), F32).at[0, FF_LANE0:FF_LANE0 + n_heads].set(fox_b_f[0])
    qp, kp = _fox_prep(proj, small, bf_pad, fox_q_gain[0].reshape(1, half),
                       fox_k_gain[0].reshape(1, half), 4, 5, tm)
    tq = _tile(s, 256)
    fv = proj[:, :, 6 * half:7 * half].reshape(bsz, s // tq, tq, n_heads, HEAD_DIM)
    vt = fv.transpose(0, 3, 1, 4, 2)
    ones = jnp.zeros((bsz, n_heads, s // tq, V_ROWS - HEAD_DIM, tq), BF16).at[:, :, :, 0, :].set(1.0)
    vt = jnp.concatenate([vt, ones], axis=3)
    o_fox = _fox_attn(qp, kp, vt, tq)

    x = _tail(x, o_gla, o_fox, even_w_out[0].astype(BF16), g1, sc2, sh2, g2,
              mlp_w1[0].astype(BF16), mlp_w2[0].astype(BF16), None, tm)

    sh1, sc1, g1, sh2, sc2, g2 = mods(1)
    (proj,) = _inproj(x, sc1, sh1, odd_w_in[0].astype(BF16), None, tm)

    groups = half // S5_GROUP_WIDTH
    nck = s // S5_CHUNK
    u = proj[:, :, :half].reshape(bsz, nck, S5_CHUNK, groups, S5_GROUP_WIDTH)
    u_g = u.transpose(3, 1, 0, 4, 2).reshape(groups, nck * bsz, S5_GROUP_WIDTH * S5_CHUNK)
    ktab, m_in, m_out, a_pack, d_vec = _s5_tables(
        s5_lam_re[0], s5_lam_im[0], s5_log_dt[0], s5_b_re[0], s5_b_im[0],
        s5_c_re[0], s5_c_im[0], s5_d[0])
    y_g = _s5(u_g, ktab, m_in, m_out, a_pack, d_vec, bsz)
    y = y_g.reshape(groups, nck, bsz, S5_GROUP_WIDTH, S5_CHUNK).transpose(2, 1, 4, 0, 3)
    y = y.reshape(bsz, s, half)

    bias_full = jnp.repeat(sgu_b_s[0].T, half // SGU_GROUPS, axis=1)
    y_sgu = _sgu(proj, sgu_w_s[0], bias_full, sgu_ln_gain[0][None, :], sgu_ln_bias[0][None, :],
                 1, 2, tm)

    x = _tail(x, y, y_sgu, odd_w_out[0].astype(BF16), g1, sc2, sh2, g2,
              mlp_w1[1].astype(BF16), mlp_w2[1].astype(BF16),
              (s5_w_glu[0].astype(BF16), s5_b_glu[0][None, :]), tm)
    return x
```

```python
import functools

import numpy as np
import jax
import jax.numpy as jnp
from jax import lax
from jax.experimental import pallas as pl
from jax.experimental.pallas import tpu as pltpu

F32 = jnp.float32
BF16 = jnp.bfloat16

EPS = 1e-6
HEAD_DIM = 64
LANES = 128
GLA_RANK = 16
GLA_TAU = 16.0
GLA_CHUNK = 64
S5_GROUP_WIDTH = 16
S5_STATE = 64
S5_CHUNK = 128
SGU_GROUPS = 8
SGU_CHUNK = 128
VMEM_LIMIT_BYTES = 56 * 1024 * 1024
NEG_BIG = -1e30
LOG2E = 1.4426950408889634


def _cparams(*sem):
    return pltpu.CompilerParams(dimension_semantics=sem, vmem_limit_bytes=VMEM_LIMIT_BYTES)


def _const_spec(shape):
    return pl.BlockSpec(shape, lambda *_: (0,) * len(shape), pipeline_mode=pl.Buffered(1))


def _dot(a, b):
    return jnp.dot(a, b, preferred_element_type=F32)


def _dot_nt(a, b):
    return lax.dot_general(a, b, (((1,), (1,)), ((), ())), preferred_element_type=F32)


def _dot_tn(a, b):
    return lax.dot_general(a, b, (((0,), (0,)), ((), ())), preferred_element_type=F32)


def _split2(x):
    hi = x.astype(BF16)
    lo = (x - hi.astype(F32)).astype(BF16)
    return hi, lo


def _split3(x):
    hi = x.astype(BF16)
    r = x - hi.astype(F32)
    mid = r.astype(BF16)
    lo = (r - mid.astype(F32)).astype(BF16)
    return hi, mid, lo


def _log_sigmoid(z):
    return jnp.minimum(z, 0.0) - jnp.log(1.0 + jnp.exp(-jnp.abs(z)))


def _norm_mod(x, sc, sh):
    ms = jnp.mean(x * x, axis=-1, keepdims=True)
    return x * lax.rsqrt(ms + EPS) * (1.0 + sc) + sh


def _ada_kernel(c_ref, w_ref, b_ref, o_ref):
    c = c_ref[...]
    ca = c * jax.nn.sigmoid(c)
    o_ref[0] = _dot(ca, w_ref[0]) + b_ref[0]


def _ada_mod(c, ada_w, ada_b):
    depth, d, n = ada_w.shape
    bsz = c.shape[0]
    tn = n // 4
    return pl.pallas_call(
        _ada_kernel,
        grid=(depth, n // tn),
        in_specs=[pl.BlockSpec((bsz, d), lambda l, j: (0, 0)),
                  pl.BlockSpec((1, d, tn), lambda l, j: (l, 0, j)),
                  pl.BlockSpec((1, 1, tn), lambda l, j: (l, 0, j))],
        out_specs=pl.BlockSpec((1, bsz, tn), lambda l, j: (l, 0, j)),
        out_shape=jax.ShapeDtypeStruct((depth, bsz, n), F32),
        compiler_params=_cparams("parallel", "parallel"),
        name="ada_mod",
    )(c, ada_w, ada_b.reshape(depth, 1, n))


def _inproj_kernel(x_ref, sc_ref, sh_ref, wm_ref, *rest, n_chunk, with_small):
    if with_small:
        ws_ref, om_ref, os_ref = rest
    else:
        (om_ref,) = rest
    h = _norm_mod(x_ref[0], sc_ref[0], sh_ref[0]).astype(BF16)
    n = wm_ref.shape[1]
    for c0 in range(0, n, n_chunk):
        om_ref[0, :, c0:c0 + n_chunk] = _dot(h, wm_ref[:, c0:c0 + n_chunk]).astype(BF16)
    if with_small:
        os_ref[0] = _dot(h, ws_ref[...])


def _inproj(x, sc, sh, w_main, w_small, tm):
    bsz, s, d = x.shape
    n = w_main.shape[1]
    with_small = w_small is not None
    in_specs = [pl.BlockSpec((1, tm, d), lambda b, i: (b, i, 0)),
                pl.BlockSpec((1, 1, d), lambda b, i: (b, 0, 0)),
                pl.BlockSpec((1, 1, d), lambda b, i: (b, 0, 0)),
                _const_spec((d, n))]
    args = [x, sc, sh, w_main]
    out_specs = [pl.BlockSpec((1, tm, n), lambda b, i: (b, i, 0))]
    out_shape = [jax.ShapeDtypeStruct((bsz, s, n), BF16)]
    if with_small:
        in_specs.append(_const_spec((d, LANES)))
        args.append(w_small)
        out_specs.append(pl.BlockSpec((1, tm, LANES), lambda b, i: (b, i, 0)))
        out_shape.append(jax.ShapeDtypeStruct((bsz, s, LANES), F32))
    return pl.pallas_call(
        functools.partial(_inproj_kernel, n_chunk=512, with_small=with_small),
        grid=(bsz, s // tm),
        in_specs=in_specs, out_specs=out_specs, out_shape=out_shape,
        compiler_params=_cparams("parallel", "parallel"),
        name="inproj",
    )(*args)


def _gla_kernel(q_ref, k_ref, v_ref, g_ref, sm_ref, wlr_ref, blr_ref, gain_ref, o_ref,
                state_ref, *, n_heads, n_chunks):
    C = GLA_CHUNK
    mid = C // 2 - 1

    @pl.when(pl.program_id(1) == 0)
    def _():
        state_ref[...] = jnp.zeros_like(state_ref)

    row = lax.broadcasted_iota(jnp.int32, (C, C), 0)
    col = lax.broadcasted_iota(jnp.int32, (C, C), 1)
    causal = col <= row
    tril = causal.astype(BF16)
    wlr = wlr_ref[...]
    blr = blr_ref[...]
    gain = gain_ref[...]

    def chunk(c, carry):
        r0 = pl.multiple_of(c * C, C)
        rows = pl.ds(r0, C)
        z = _dot(sm_ref[0, rows, :], wlr) + blr
        lc = _log_sigmoid(z) * (1.0 / GLA_TAU)
        hi, lo = _split2(lc)
        bc = _dot(tril, hi) + _dot(tril, lo)
        m = bc[mid:mid + 1, :]
        bl = bc[C - 1:C, :]
        qt = q_ref[0, rows, :].astype(F32) * (HEAD_DIM ** -0.5) * jnp.exp(bc - m)
        kt = k_ref[0, rows, :].astype(F32) * jnp.exp(m - bc)
        qin = (qt * jnp.exp(m)).astype(BF16)
        kst = (kt * jnp.exp(bl - m)).astype(BF16)
        qt = qt.astype(BF16)
        kt = kt.astype(BF16)
        ebl = jnp.exp(bl)
        v = v_ref[0, rows, :]
        outs = []
        for h in range(n_heads):
            sl = slice(h * HEAD_DIM, (h + 1) * HEAD_DIM)
            a = jnp.where(causal, _dot_nt(qt[:, sl], kt[:, sl]), 0.0)
            st = state_ref[h]
            o = _dot(a.astype(BF16), v[:, sl]) + _dot_nt(qin[:, sl], st.astype(BF16))
            state_ref[h] = st * ebl[:, sl] + _dot_tn(v[:, sl], kst[:, sl])
            ms = jnp.mean(o * o, axis=-1, keepdims=True)
            outs.append(o * lax.rsqrt(ms + EPS))
        o_all = jnp.concatenate(outs, axis=-1)
        g = g_ref[0, rows, :].astype(F32)
        o_ref[0, rows, :] = (o_all * gain * (g * jax.nn.sigmoid(g))).astype(BF16)
        return carry

    lax.fori_loop(0, n_chunks, chunk, 0)


def _gla(proj, small, wlr_pad, blr, gain, tt):
    bsz, s, _ = proj.shape
    w = wlr_pad.shape[1]
    n_heads = w // HEAD_DIM

    def col(j):
        return pl.BlockSpec((1, tt, w), lambda b, i, j=j: (b, i, j))

    return pl.pallas_call(
        functools.partial(_gla_kernel, n_heads=n_heads, n_chunks=tt // GLA_CHUNK),
        grid=(bsz, s // tt),
        in_specs=[col(0), col(1), col(2), col(3),
                  pl.BlockSpec((1, tt, LANES), lambda b, i: (b, i, 0)),
                  _const_spec((LANES, w)), _const_spec((1, w)), _const_spec((1, w))],
        out_specs=pl.BlockSpec((1, tt, w), lambda b, i: (b, i, 0)),
        out_shape=jax.ShapeDtypeStruct((bsz, s, w), BF16),
        scratch_shapes=[pltpu.VMEM((n_heads, HEAD_DIM, HEAD_DIM), F32)],
        compiler_params=_cparams("parallel", "arbitrary"),
        name="gla",
    )(proj, proj, proj, proj, small, wlr_pad, blr, gain)


FF_LANE0 = GLA_RANK


def _fox_prep_kernel(q_ref, k_ref, sm_ref, bf_ref, qg_ref, kg_ref, tril_ref, bd_ref,
                     sq_ref, sk_ref, cq_ref, ck_ref, qo_ref, ko_ref, co_ref, carry_ref, *, n_heads):
    @pl.when(pl.program_id(1) == 0)
    def _():
        carry_ref[...] = jnp.zeros_like(carry_ref)

    sm = sm_ref[0]
    tt = sm.shape[0]
    lane = lax.broadcasted_iota(jnp.int32, (tt, LANES), 1)
    is_ff = (lane >= FF_LANE0) & (lane < FF_LANE0 + n_heads)
    ls = jnp.where(is_ff, _log_sigmoid(sm + bf_ref[...]), 0.0)
    tril = tril_ref[...]
    p0, p1, p2 = _split3(ls)
    cum = carry_ref[0:1, :] + (_dot(tril, p0) + _dot(tril, p1) + _dot(tril, p2))
    carry_ref[0:1, :] = cum[tt - 1:tt, :]
    co_ref[0] = cum
    c0, c1, c2 = _split3(cum * LOG2E)
    ex_q = _dot(c0, sq_ref[0]) + _dot(c1, sq_ref[1]) + _dot(c2, sq_ref[2]) + cq_ref[...]
    ex_k = _dot(c0, sk_ref[0]) + _dot(c1, sk_ref[1]) + _dot(c2, sk_ref[2]) + ck_ref[...]
    bd = bd_ref[...]
    low = lane < HEAD_DIM

    def normed(x_ref, gain_ref, scale, pair):
        x = x_ref[0, :, pair * LANES:(pair + 1) * LANES].astype(F32)
        hi, lo = _split2(x * x)
        ms = _dot(hi, bd) + _dot(lo, bd)
        return x * lax.rsqrt(ms + EPS) * (gain_ref[:, pair * LANES:(pair + 1) * LANES] * scale)

    for pair in range(n_heads // 2):
        qn = normed(q_ref, qg_ref, HEAD_DIM ** -0.5 * LOG2E, pair)
        kn = normed(k_ref, kg_ref, 1.0, pair)
        for odd in range(2):
            h = 2 * pair + odd
            if odd:
                qh = pltpu.roll(qn, HEAD_DIM, axis=1)
                kh = pltpu.roll(kn, HEAD_DIM, axis=1)
            else:
                qh, kh = qn, kn
            qo_ref[0, h] = jnp.where(low, qh, ex_q[:, h * LANES:(h + 1) * LANES]).astype(BF16)
            ko_ref[0, h] = jnp.where(low, kh, ex_k[:, h * LANES:(h + 1) * LANES]).astype(BF16)


def _fox_prep_consts(n_heads):
    sq = np.zeros((3, LANES, n_heads * LANES), np.float32)
    sk = np.zeros((3, LANES, n_heads * LANES), np.float32)
    cq = np.zeros((1, n_heads * LANES), np.float32)
    ck = np.zeros((1, n_heads * LANES), np.float32)
    for h in range(n_heads):
        base = h * LANES + HEAD_DIM
        for r in range(3):
            sq[r, FF_LANE0 + h, base + r] = 1.0
            sk[r, FF_LANE0 + h, base + 3 + r] = -1.0
            cq[0, base + 3 + r] = 1.0
            ck[0, base + r] = 1.0
    bd = np.kron(np.eye(LANES // HEAD_DIM, dtype=np.float32),
                 np.full((HEAD_DIM, HEAD_DIM), 1.0 / HEAD_DIM, np.float32))
    return (jnp.asarray(sq, BF16), jnp.asarray(sk, BF16), jnp.asarray(cq), jnp.asarray(ck),
            jnp.asarray(bd, BF16))


def _fox_prep(proj, small, bf_pad, q_gain, k_gain, q_col, k_col, tt):
    bsz, s, _ = proj.shape
    w = q_gain.shape[1]
    n_heads = w // HEAD_DIM
    sq, sk, cq, ck, bd = _fox_prep_consts(n_heads)
    tril = jnp.asarray(np.tril(np.ones((tt, tt), np.float32)), BF16)
    out_spec = pl.BlockSpec((1, n_heads, tt, LANES), lambda b, i: (b, 0, i, 0))
    out_sds = jax.ShapeDtypeStruct((bsz, n_heads, s, LANES), BF16)
    return pl.pallas_call(
        functools.partial(_fox_prep_kernel, n_heads=n_heads),
        grid=(bsz, s // tt),
        in_specs=[pl.BlockSpec((1, tt, w), lambda b, i: (b, i, q_col)),
                  pl.BlockSpec((1, tt, w), lambda b, i: (b, i, k_col)),
                  pl.BlockSpec((1, tt, LANES), lambda b, i: (b, i, 0)),
                  _const_spec((1, LANES)), _const_spec((1, w)), _const_spec((1, w)),
                  _const_spec((tt, tt)), _const_spec((LANES, LANES)),
                  _const_spec(sq.shape), _const_spec(sk.shape),
                  _const_spec(cq.shape), _const_spec(ck.shape)],
        out_specs=[out_spec, out_spec, pl.BlockSpec((1, tt, LANES), lambda b, i: (b, i, 0))],
        out_shape=[out_sds, out_sds, jax.ShapeDtypeStruct((bsz, s, LANES), F32)],
        scratch_shapes=[pltpu.VMEM((8, LANES), F32)],
        compiler_params=_cparams("parallel", "arbitrary"),
        name="fox_prep",
    )(proj, proj, small, bf_pad, q_gain, k_gain, tril, bd, sq, sk, cq, ck)


V_ROWS = HEAD_DIM + 8


SUB = 256
ZERO_MARGIN = 160.0


def _fox_attn_kernel(cq_ref, ce_ref, thr_ref, q_ref, k_ref, vt_ref, o_ref, m_ref, acc_ref, s_ref,
                     tri_ref, *, n_sub):
    b = pl.program_id(0)
    pair = pl.program_id(1)
    i = pl.program_id(2)
    row = lax.broadcasted_iota(jnp.int32, (SUB, SUB), 0)
    col = lax.broadcasted_iota(jnp.int32, (SUB, SUB), 1)
    tri_ref[...] = jnp.where(row <= col, 0.0, NEG_BIG)

    def blocks(r):
        a = i * n_sub + r
        return a, jnp.maximum(a - 1, 0)

    for r in range(n_sub):
        a, jp = blocks(r)
        for hh in range(2):
            q = q_ref[0, hh, r * SUB:(r + 1) * SUB, :]
            s_ref[r, hh, :SUB, :] = _dot_nt(k_ref[0, hh, pl.ds(pl.multiple_of(jp * SUB, SUB), SUB), :], q)
            s_ref[r, hh, SUB:, :] = _dot_nt(k_ref[0, hh, pl.ds(pl.multiple_of(a * SUB, SUB), SUB), :], q)
    for r in range(n_sub):
        a, jp = blocks(r)
        no_prev = jnp.where(a == 0, NEG_BIG, 0.0)
        for hh in range(2):
            st = s_ref[r, hh, :SUB, :] + no_prev
            sd = s_ref[r, hh, SUB:, :] + tri_ref[...]
            m = jnp.maximum(jnp.max(st, axis=0, keepdims=True), jnp.max(sd, axis=0, keepdims=True))
            pt = jnp.exp2((st - m).astype(BF16))
            pd = jnp.exp2((sd - m).astype(BF16))
            acc_ref[r, hh] = _dot(vt_ref[0, hh, jp], pt) + _dot(vt_ref[0, hh, a], pd)
            m_ref[r, hh, 0:1, :] = m

    for r in range(n_sub):
        a = i * n_sub + r
        outs = []
        for hh in range(2):
            h = 2 * pair + hh
            cq = cq_ref[b, h, a]
            thr = thr_ref[h]
            q = q_ref[0, hh, r * SUB:(r + 1) * SUB, :]

            def cond(c):
                j = c[0]
                return (j >= 0) & (cq - ce_ref[b, h, jnp.maximum(j, 0)] >= thr)

            def body(c):
                j, m, acc = c
                kb = k_ref[0, hh, pl.ds(pl.multiple_of(j * SUB, SUB), SUB), :]
                s = _dot_nt(kb, q)
                m_new = jnp.maximum(m, jnp.max(s, axis=0, keepdims=True))
                p = jnp.exp2((s - m_new).astype(BF16))
                return j - 1, m_new, jnp.exp2(m - m_new) * acc + _dot(vt_ref[0, hh, j], p)

            _, _, acc = lax.while_loop(cond, body, (a - 2, m_ref[r, hh, 0:1, :], acc_ref[r, hh]))
            outs.append(acc[:HEAD_DIM] / acc[HEAD_DIM:HEAD_DIM + 1])
        o_ref[0, r * SUB:(r + 1) * SUB, :] = jnp.concatenate(outs, axis=0).T.astype(BF16)


def _fox_attn(cq, ce, thr, qp, kp, vt, tq):
    bsz, n_heads, s, _ = qp.shape
    nkb = s // SUB
    n_sub = tq // SUB
    grid_spec = pltpu.PrefetchScalarGridSpec(
        num_scalar_prefetch=3,
        grid=(bsz, n_heads // 2, s // tq),
        in_specs=[pl.BlockSpec((1, 2, tq, LANES), lambda b, p, i, *_: (b, p, i, 0)),
                  pl.BlockSpec((1, 2, s, LANES), lambda b, p, i, *_: (b, p, 0, 0)),
                  pl.BlockSpec((1, 2, nkb, V_ROWS, SUB), lambda b, p, i, *_: (b, p, 0, 0, 0))],
        out_specs=pl.BlockSpec((1, tq, LANES), lambda b, p, i, *_: (b, i, p)),
        scratch_shapes=[pltpu.VMEM((n_sub, 2, 8, SUB), F32),
                        pltpu.VMEM((n_sub, 2, V_ROWS, SUB), F32),
                        pltpu.VMEM((n_sub, 2, 2 * SUB, SUB), F32),
                        pltpu.VMEM((SUB, SUB), F32)])
    return pl.pallas_call(
        functools.partial(_fox_attn_kernel, n_sub=n_sub),
        grid_spec=grid_spec,
        out_shape=jax.ShapeDtypeStruct((bsz, s, n_heads * HEAD_DIM), BF16),
        compiler_params=_cparams("parallel", "parallel", "arbitrary"),
        name="fox_attn",
    )(cq, ce, thr, qp, kp, vt)


def _fox_skip_tables(cum, q_gain, k_gain, n_heads):
    bsz, s, _ = cum.shape
    c2 = cum[:, :, FF_LANE0:FF_LANE0 + n_heads] * LOG2E
    cq = c2[:, ::SUB, :].transpose(0, 2, 1)
    ce = c2[:, SUB - 1::SUB, :].transpose(0, 2, 1)
    bound = (jnp.max(jnp.abs(q_gain), axis=-1) * jnp.max(jnp.abs(k_gain), axis=-1)
             * (HEAD_DIM ** 0.5) * LOG2E)
    thr = -(2.0 * 1.02 * bound + ZERO_MARGIN)
    return cq, ce, thr


def _s5_kernel(u_ref, kt_ref, min_ref, mout_ref, a_ref, d_ref, o_ref,
               t_ref, v_ref, xp_ref, *, bsz, n_col):
    L = S5_CHUNK
    W = S5_GROUP_WIDTH
    nc = u_ref.shape[1]
    row = lax.broadcasted_iota(jnp.int32, (L, L), 0)
    colm = lax.broadcasted_iota(jnp.int32, (L, L), 1)
    causal = colm >= row

    def build(j, carry):
        kj = kt_ref[0, j]
        for i in range(W):
            tile = jnp.broadcast_to(kj[i:i + 1, :], (L, L))
            tile = pltpu.roll(tile, 0, axis=1, stride=1, stride_axis=0)
            t_ref[pl.ds(pl.multiple_of(j * L, L), L), i * L:(i + 1) * L] = (
                jnp.where(causal, tile, 0.0).astype(BF16))
        return carry

    lax.fori_loop(0, W, build, 0)

    u = u_ref[0]
    v_ref[...] = _dot(u, min_ref[0])
    a1 = a_ref[0, 0:1, :]
    a2 = a_ref[0, 1:2, :]

    def scan(c, x):
        rows = pl.ds(pl.multiple_of(c * bsz, bsz), bsz)
        xp_ref[rows, :] = x
        return x * a1 + pltpu.roll(x, S5_STATE, axis=1) * a2 + v_ref[rows, :]

    lax.fori_loop(0, nc // bsz, scan, jnp.zeros((bsz, 2 * S5_STATE), F32))

    xp = xp_ref[...].astype(BF16)
    for c0 in range(0, W * L, n_col):
        cs = slice(c0, c0 + n_col)
        y = (_dot(u, t_ref[:, cs]) + _dot(xp, mout_ref[0, :, cs])
             + d_ref[0, :, cs] * u[:, cs].astype(F32))
        o_ref[0, :, cs] = jax.nn.gelu(y).astype(BF16)


def _s5(u_g, ktab, m_in, m_out, a_pack, d_vec, bsz):
    groups, nc, wl = u_g.shape
    return pl.pallas_call(
        functools.partial(_s5_kernel, bsz=bsz, n_col=512),
        grid=(groups,),
        in_specs=[pl.BlockSpec((1, nc, wl), lambda g: (g, 0, 0)),
                  pl.BlockSpec((1, S5_GROUP_WIDTH, S5_GROUP_WIDTH, S5_CHUNK), lambda g: (g, 0, 0, 0)),
                  pl.BlockSpec((1, wl, 2 * S5_STATE), lambda g: (g, 0, 0)),
                  pl.BlockSpec((1, 2 * S5_STATE, wl), lambda g: (g, 0, 0)),
                  pl.BlockSpec((1, 2, 2 * S5_STATE), lambda g: (g, 0, 0)),
                  pl.BlockSpec((1, 1, wl), lambda g: (g, 0, 0))],
        out_specs=pl.BlockSpec((1, nc, wl), lambda g: (g, 0, 0)),
        out_shape=jax.ShapeDtypeStruct((groups, nc, wl), BF16),
        scratch_shapes=[pltpu.VMEM((wl, wl), BF16),
                        pltpu.VMEM((nc, 2 * S5_STATE), F32),
                        pltpu.VMEM((nc, 2 * S5_STATE), F32)],
        compiler_params=_cparams("parallel"),
        name="s5",
    )(u_g, ktab, m_in, m_out, a_pack, d_vec)


def _s5_tables(lam_re, lam_im, log_dt, b_re, b_im, c_re, c_im, d_skip):
    L = S5_CHUNK
    dt = jnp.exp(log_dt)[:, None]
    mag = jnp.exp(lam_re * dt)
    ang = lam_im * dt
    ar, ai = mag * jnp.cos(ang), mag * jnp.sin(ang)
    den = lam_re * lam_re + lam_im * lam_im
    cr = ((ar - 1.0) * lam_re + ai * lam_im) / den
    ci = (ai * lam_re - (ar - 1.0) * lam_im) / den
    bbr = cr[..., None] * b_re - ci[..., None] * b_im
    bbi = cr[..., None] * b_im + ci[..., None] * b_re
    pr, pi = jnp.ones_like(ar)[None], jnp.zeros_like(ar)[None]
    sr, si = ar, ai
    n = 1
    while n <= L:
        pr, pi = (jnp.concatenate([pr, pr * sr - pi * si], 0),
                  jnp.concatenate([pi, pr * si + pi * sr], 0))
        sr, si = sr * sr - si * si, 2.0 * sr * si
        n *= 2
    pr, pi = pr[:L + 1], pi[:L + 1]
    hp = lax.Precision.HIGHEST
    er = c_re[None] * pr[:L, :, None, :] - c_im[None] * pi[:L, :, None, :]
    ei = -(c_re[None] * pi[:L, :, None, :] + c_im[None] * pr[:L, :, None, :])
    ktab = (jnp.einsum('lgip,gpj->gjil', er, bbr, precision=hp)
            + jnp.einsum('lgip,gpj->gjil', ei, bbi, precision=hp))
    qr, qi = pr[:L][::-1], pi[:L][::-1]
    min_re = qr[:, :, :, None] * bbr[None] - qi[:, :, :, None] * bbi[None]
    min_im = qr[:, :, :, None] * bbi[None] + qi[:, :, :, None] * bbr[None]
    m_in = jnp.concatenate([min_re, min_im], axis=2)
    m_in = m_in.transpose(1, 3, 0, 2).reshape(m_in.shape[1], -1, 2 * S5_STATE)
    tr, ti = pr[1:L + 1], pi[1:L + 1]
    mo_re = c_re[None] * tr[:, :, None, :] - c_im[None] * ti[:, :, None, :]
    mo_im = -(c_re[None] * ti[:, :, None, :] + c_im[None] * tr[:, :, None, :])
    m_out = jnp.concatenate([mo_re, mo_im], axis=3)
    m_out = m_out.transpose(1, 3, 2, 0).reshape(m_out.shape[1], 2 * S5_STATE, -1)
    a_pack = jnp.stack([jnp.concatenate([pr[L], pr[L]], -1),
                        jnp.concatenate([-pi[L], pi[L]], -1)], axis=1)
    d_vec = jnp.repeat(d_skip, L, axis=-1)[:, None, :]
    return ktab, m_in.astype(BF16), m_out.astype(BF16), a_pack, d_vec


def _sgu_kernel(u_ref, v_ref, w_ref, bias_ref, lg_ref, lb_ref, o_ref, *, n_chunks):
    L = SGU_CHUNK
    row = lax.broadcasted_iota(jnp.int32, (L, L), 0)
    colm = lax.broadcasted_iota(jnp.int32, (L, L), 1)
    causal = colm <= row
    v = jax.nn.gelu(v_ref[0].astype(F32))
    mu = jnp.mean(v, axis=-1, keepdims=True)
    vc = v - mu
    var = jnp.mean(vc * vc, axis=-1, keepdims=True)
    vn = (vc * lax.rsqrt(var + EPS) * lg_ref[...] + lb_ref[...]).astype(BF16)
    gw = vn.shape[1] // SGU_GROUPS
    ws = [jnp.where(causal, w_ref[g], 0.0).astype(BF16) for g in range(SGU_GROUPS)]
    for c in range(n_chunks):
        rows = slice(c * L, (c + 1) * L)
        mixed = jnp.concatenate(
            [_dot(ws[g], vn[rows, g * gw:(g + 1) * gw]) for g in range(SGU_GROUPS)], axis=-1)
        u = jax.nn.gelu(u_ref[0, rows, :].astype(F32))
        o_ref[0, rows, :] = (u * (mixed + bias_ref[...])).astype(BF16)


def _sgu(proj, w_s, bias_full, ln_gain, ln_bias, u_col, v_col, tt):
    bsz, s, _ = proj.shape
    w = ln_gain.shape[1]
    return pl.pallas_call(
        functools.partial(_sgu_kernel, n_chunks=tt // SGU_CHUNK),
        grid=(bsz, s // tt),
        in_specs=[pl.BlockSpec((1, tt, w), lambda b, i: (b, i, u_col)),
                  pl.BlockSpec((1, tt, w), lambda b, i: (b, i, v_col)),
                  _const_spec(w_s.shape), _const_spec(bias_full.shape),
                  _const_spec((1, w)), _const_spec((1, w))],
        out_specs=pl.BlockSpec((1, tt, w), lambda b, i: (b, i, 0)),
        out_shape=jax.ShapeDtypeStruct((bsz, s, w), BF16),
        compiler_params=_cparams("parallel", "parallel"),
        name="sgu",
    )(proj, proj, w_s, bias_full, ln_gain, ln_bias)


def _tail_kernel(x_ref, a_ref, b_ref, wo_ref, g1_ref, sc_ref, sh_ref, g2_ref, w1_ref, w2_ref,
                 *rest, f_chunk, with_glu):
    if with_glu:
        wg_ref, bg_ref, o_ref = rest
    else:
        (o_ref,) = rest
    a = a_ref[0]
    half = a.shape[1]
    if with_glu:
        af = a.astype(F32)
        a = (af * jax.nn.sigmoid(_dot(a, wg_ref[...]) + bg_ref[...])).astype(BF16)
    y = _dot(a, wo_ref[:half, :]) + _dot(b_ref[0], wo_ref[half:, :])
    x1 = x_ref[0] + g1_ref[0] * y
    h = _norm_mod(x1, sc_ref[0], sh_ref[0]).astype(BF16)
    f = w1_ref.shape[1]
    acc = None
    for f0 in range(0, f, f_chunk):
        hid = jnp.maximum(_dot(h, w1_ref[:, f0:f0 + f_chunk]), 0.0)
        part = _dot((hid * hid).astype(BF16), w2_ref[f0:f0 + f_chunk, :])
        acc = part if acc is None else acc + part
    o_ref[0] = x1 + g2_ref[0] * acc


def _tail(x, mix_a, mix_b, w_out, g1, sc, sh, g2, w1, w2, glu, tm):
    bsz, s, d = x.shape
    half = mix_a.shape[-1]
    f = w1.shape[1]
    row = pl.BlockSpec((1, 1, d), lambda b, i: (b, 0, 0))
    in_specs = [pl.BlockSpec((1, tm, d), lambda b, i: (b, i, 0)),
                pl.BlockSpec((1, tm, half), lambda b, i: (b, i, 0)),
                pl.BlockSpec((1, tm, half), lambda b, i: (b, i, 0)),
                _const_spec((d, d)), row, row, row, row,
                _const_spec((d, f)), _const_spec((f, d))]
    args = [x, mix_a, mix_b, w_out, g1, sc, sh, g2, w1, w2]
    if glu is not None:
        in_specs += [_const_spec((half, half)), _const_spec((1, half))]
        args += list(glu)
    return pl.pallas_call(
        functools.partial(_tail_kernel, f_chunk=1024, with_glu=glu is not None),
        grid=(bsz, s // tm),
        in_specs=in_specs,
        out_specs=pl.BlockSpec((1, tm, d), lambda b, i: (b, i, 0)),
        out_shape=jax.ShapeDtypeStruct((bsz, s, d), F32),
        compiler_params=_cparams("parallel", "parallel"),
        name="tail",
    )(*args)


def _tile(s, want):
    t = min(s, want)
    assert s % t == 0, (s, t)
    return t


def kernel(x, c, ada_w, ada_b, even_w_in, even_w_out, gla_w_lr, gla_b_lr, gla_gain, fox_b_f, fox_q_gain, fox_k_gain, odd_w_in, odd_w_out, s5_lam_re, s5_lam_im, s5_log_dt, s5_b_re, s5_b_im, s5_c_re, s5_c_im, s5_d, s5_w_glu, s5_b_glu, sgu_ln_gain, sgu_ln_bias, sgu_w_s, sgu_b_s, mlp_w1, mlp_w2):
    bsz, s, d = x.shape
    half = d // 2
    n_heads = half // HEAD_DIM
    tm = _tile(s, 512)

    mod = _ada_mod(c, ada_w, ada_b)

    def mods(layer):
        return [m[:, None, :] for m in jnp.split(mod[layer], 6, axis=-1)]

    sh1, sc1, g1, sh2, sc2, g2 = mods(0)
    w_in = even_w_in[0]
    o_lr = 4 * half
    o_fq = o_lr + GLA_RANK
    o_ff = o_fq + 3 * half
    w_main = jnp.concatenate([w_in[:, :o_lr], w_in[:, o_fq:o_ff]], axis=1).astype(BF16)
    w_small = jnp.concatenate(
        [w_in[:, o_lr:o_fq], w_in[:, o_ff:],
         jnp.zeros((d, LANES - GLA_RANK - n_heads), w_in.dtype)], axis=1).astype(BF16)
    proj, small = _inproj(x, sc1, sh1, w_main, w_small, tm)

    wlr_pad = jnp.concatenate([gla_w_lr[0], jnp.zeros((LANES - GLA_RANK, half), F32)], axis=0)
    o_gla = _gla(proj, small, wlr_pad, gla_b_lr[0][None, :], gla_gain[0].reshape(1, half), tm)

    bf_pad = jnp.zeros((1, LANES), F32).at[0, FF_LANE0:FF_LANE0 + n_heads].set(fox_b_f[0])
    qp, kp, cum = _fox_prep(proj, small, bf_pad, fox_q_gain[0].reshape(1, half),
                            fox_k_gain[0].reshape(1, half), 4, 5, tm)
    cq, ce, thr = _fox_skip_tables(cum, fox_q_gain[0], fox_k_gain[0], n_heads)
    nkb = s // SUB
    fv = proj[:, :, 6 * half:7 * half].reshape(bsz, nkb, SUB, n_heads, HEAD_DIM)
    vt = fv.transpose(0, 3, 1, 4, 2)
    ones = jnp.zeros((bsz, n_heads, nkb, V_ROWS - HEAD_DIM, SUB), BF16).at[:, :, :, 0, :].set(1.0)
    vt = jnp.concatenate([vt, ones], axis=3)
    o_fox = _fox_attn(cq, ce, thr, qp, kp, vt, _tile(s, 1024))

    x = _tail(x, o_gla, o_fox, even_w_out[0].astype(BF16), g1, sc2, sh2, g2,
              mlp_w1[0].astype(BF16), mlp_w2[0].astype(BF16), None, tm)

    sh1, sc1, g1, sh2, sc2, g2 = mods(1)
    (proj,) = _inproj(x, sc1, sh1, odd_w_in[0].astype(BF16), None, tm)

    groups = half // S5_GROUP_WIDTH
    nck = s // S5_CHUNK
    u = proj[:, :, :half].reshape(bsz, nck, S5_CHUNK, groups, S5_GROUP_WIDTH)
    u_g = u.transpose(3, 1, 0, 4, 2).reshape(groups, nck * bsz, S5_GROUP_WIDTH * S5_CHUNK)
    ktab, m_in, m_out, a_pack, d_vec = _s5_tables(
        s5_lam_re[0], s5_lam_im[0], s5_log_dt[0], s5_b_re[0], s5_b_im[0],
        s5_c_re[0], s5_c_im[0], s5_d[0])
    y_g = _s5(u_g, ktab, m_in, m_out, a_pack, d_vec, bsz)
    y = y_g.reshape(groups, nck, bsz, S5_GROUP_WIDTH, S5_CHUNK).transpose(2, 1, 4, 0, 3)
    y = y.reshape(bsz, s, half)

    bias_full = jnp.repeat(sgu_b_s[0].T, half // SGU_GROUPS, axis=1)
    y_sgu = _sgu(proj, sgu_w_s[0], bias_full, sgu_ln_gain[0][None, :], sgu_ln_bias[0][None, :],
                 1, 2, tm)

    x = _tail(x, y, y_sgu, odd_w_out[0].astype(BF16), g1, sc2, sh2, g2,
              mlp_w1[1].astype(BF16), mlp_w2[1].astype(BF16),
              (s5_w_glu[0].astype(BF16), s5_b_glu[0][None, :]), tm)
    return x
```

```python
import functools

import numpy as np
import jax
import jax.numpy as jnp
from jax import lax
from jax.experimental import pallas as pl
from jax.experimental.pallas import tpu as pltpu

F32 = jnp.float32
BF16 = jnp.bfloat16

EPS = 1e-6
HEAD_DIM = 64
LANES = 128
GLA_RANK = 16
GLA_TAU = 16.0
GLA_CHUNK = 64
S5_GROUP_WIDTH = 16
S5_STATE = 64
S5_CHUNK = 64
SGU_GROUPS = 8
SGU_CHUNK = 128
VMEM_LIMIT_BYTES = 56 * 1024 * 1024
NEG_BIG = -1e30
LOG2E = 1.4426950408889634


def _cparams(*sem):
    return pltpu.CompilerParams(dimension_semantics=sem, vmem_limit_bytes=VMEM_LIMIT_BYTES)


def _const_spec(shape):
    return pl.BlockSpec(shape, lambda *_: (0,) * len(shape), pipeline_mode=pl.Buffered(1))


def _dot(a, b):
    return jnp.dot(a, b, preferred_element_type=F32)


def _dot_nt(a, b):
    return lax.dot_general(a, b, (((1,), (1,)), ((), ())), preferred_element_type=F32)


def _dot_tn(a, b):
    return lax.dot_general(a, b, (((0,), (0,)), ((), ())), preferred_element_type=F32)


def _split2(x):
    hi = x.astype(BF16)
    lo = (x - hi.astype(F32)).astype(BF16)
    return hi, lo


def _split3(x):
    hi = x.astype(BF16)
    r = x - hi.astype(F32)
    mid = r.astype(BF16)
    lo = (r - mid.astype(F32)).astype(BF16)
    return hi, mid, lo


def _log_sigmoid(z):
    return jnp.minimum(z, 0.0) - jnp.log(1.0 + jnp.exp(-jnp.abs(z)))


def _norm_mod(x, sc, sh):
    ms = jnp.mean(x * x, axis=-1, keepdims=True)
    return x * lax.rsqrt(ms + EPS) * (1.0 + sc) + sh


def _ada_kernel(c_ref, w_ref, b_ref, o_ref):
    c = c_ref[...]
    ca = c * jax.nn.sigmoid(c)
    o_ref[0] = _dot(ca, w_ref[0]) + b_ref[0]


def _ada_mod(c, ada_w, ada_b):
    depth, d, n = ada_w.shape
    bsz = c.shape[0]
    tn = n // 4
    return pl.pallas_call(
        _ada_kernel,
        grid=(depth, n // tn),
        in_specs=[pl.BlockSpec((bsz, d), lambda l, j: (0, 0)),
                  pl.BlockSpec((1, d, tn), lambda l, j: (l, 0, j)),
                  pl.BlockSpec((1, 1, tn), lambda l, j: (l, 0, j))],
        out_specs=pl.BlockSpec((1, bsz, tn), lambda l, j: (l, 0, j)),
        out_shape=jax.ShapeDtypeStruct((depth, bsz, n), F32),
        compiler_params=_cparams("parallel", "parallel"),
        name="ada_mod",
    )(c, ada_w, ada_b.reshape(depth, 1, n))


def _inproj_kernel(x_ref, sc_ref, sh_ref, wm_ref, *rest, n_chunk, with_small):
    if with_small:
        ws_ref, om_ref, os_ref = rest
    else:
        (om_ref,) = rest
    h = _norm_mod(x_ref[0], sc_ref[0], sh_ref[0]).astype(BF16)
    n = wm_ref.shape[1]
    for c0 in range(0, n, n_chunk):
        om_ref[0, :, c0:c0 + n_chunk] = _dot(h, wm_ref[:, c0:c0 + n_chunk]).astype(BF16)
    if with_small:
        os_ref[0] = _dot(h, ws_ref[...])


def _inproj(x, sc, sh, w_main, w_small, tm):
    bsz, s, d = x.shape
    n = w_main.shape[1]
    with_small = w_small is not None
    in_specs = [pl.BlockSpec((1, tm, d), lambda b, i: (b, i, 0)),
                pl.BlockSpec((1, 1, d), lambda b, i: (b, 0, 0)),
                pl.BlockSpec((1, 1, d), lambda b, i: (b, 0, 0)),
                _const_spec((d, n))]
    args = [x, sc, sh, w_main]
    out_specs = [pl.BlockSpec((1, tm, n), lambda b, i: (b, i, 0))]
    out_shape = [jax.ShapeDtypeStruct((bsz, s, n), BF16)]
    if with_small:
        in_specs.append(_const_spec((d, LANES)))
        args.append(w_small)
        out_specs.append(pl.BlockSpec((1, tm, LANES), lambda b, i: (b, i, 0)))
        out_shape.append(jax.ShapeDtypeStruct((bsz, s, LANES), F32))
    return pl.pallas_call(
        functools.partial(_inproj_kernel, n_chunk=512, with_small=with_small),
        grid=(bsz, s // tm),
        in_specs=in_specs, out_specs=out_specs, out_shape=out_shape,
        compiler_params=_cparams("parallel", "parallel"),
        name="inproj",
    )(*args)


def _gla_kernel(q_ref, k_ref, v_ref, g_ref, sm_ref, wlr_ref, blr_ref, gain_ref, tril_ref, o_ref,
                state_ref, *, n_heads, n_chunks):
    C = GLA_CHUNK
    mid = C // 2 - 1

    @pl.when(pl.program_id(1) == 0)
    def _():
        state_ref[...] = jnp.zeros_like(state_ref)

    row = lax.broadcasted_iota(jnp.int32, (C, C), 0)
    col = lax.broadcasted_iota(jnp.int32, (C, C), 1)
    causal = col <= row
    z = _dot(sm_ref[0], wlr_ref[...]) + blr_ref[...]
    lc = _log_sigmoid(z) * (1.0 / GLA_TAU)
    hi, lo = _split2(lc)
    tril = tril_ref[...]
    bc_all = _dot(tril, hi) + _dot(tril, lo)
    gain = gain_ref[...]
    states = [state_ref[h] for h in range(n_heads)]
    for c in range(n_chunks):
        rows = slice(c * C, (c + 1) * C)
        bc = bc_all[rows]
        m = bc[mid:mid + 1, :]
        bl = bc[C - 1:C, :]
        qt = q_ref[0, rows, :].astype(F32) * (HEAD_DIM ** -0.5) * jnp.exp(bc - m)
        kt = k_ref[0, rows, :].astype(F32) * jnp.exp(m - bc)
        qin = (qt * jnp.exp(m)).astype(BF16)
        kst = (kt * jnp.exp(bl - m)).astype(BF16)
        qt = qt.astype(BF16)
        kt = kt.astype(BF16)
        ebl = jnp.exp(bl)
        v = v_ref[0, rows, :]
        outs = []
        for h in range(n_heads):
            sl = slice(h * HEAD_DIM, (h + 1) * HEAD_DIM)
            a = jnp.where(causal, _dot_nt(qt[:, sl], kt[:, sl]), 0.0)
            o = _dot(a.astype(BF16), v[:, sl]) + _dot_nt(qin[:, sl], states[h].astype(BF16))
            states[h] = states[h] * ebl[:, sl] + _dot_tn(v[:, sl], kst[:, sl])
            ms = jnp.mean(o * o, axis=-1, keepdims=True)
            outs.append(o * lax.rsqrt(ms + EPS))
        o_all = jnp.concatenate(outs, axis=-1)
        g = g_ref[0, rows, :].astype(F32)
        o_ref[0, rows, :] = (o_all * gain * (g * jax.nn.sigmoid(g))).astype(BF16)
    for h in range(n_heads):
        state_ref[h] = states[h]


def _gla(proj, small, wlr_pad, blr, gain, tt):
    bsz, s, _ = proj.shape
    w = wlr_pad.shape[1]
    n_heads = w // HEAD_DIM

    def col(j):
        return pl.BlockSpec((1, tt, w), lambda b, i, j=j: (b, i, j))

    n_chunks = tt // GLA_CHUNK
    tril = jnp.asarray(np.kron(np.eye(n_chunks, dtype=np.float32),
                               np.tril(np.ones((GLA_CHUNK, GLA_CHUNK), np.float32))), BF16)
    return pl.pallas_call(
        functools.partial(_gla_kernel, n_heads=n_heads, n_chunks=n_chunks),
        grid=(bsz, s // tt),
        in_specs=[col(0), col(1), col(2), col(3),
                  pl.BlockSpec((1, tt, LANES), lambda b, i: (b, i, 0)),
                  _const_spec((LANES, w)), _const_spec((1, w)), _const_spec((1, w)),
                  _const_spec((tt, tt))],
        out_specs=pl.BlockSpec((1, tt, w), lambda b, i: (b, i, 0)),
        out_shape=jax.ShapeDtypeStruct((bsz, s, w), BF16),
        scratch_shapes=[pltpu.VMEM((n_heads, HEAD_DIM, HEAD_DIM), F32)],
        compiler_params=_cparams("parallel", "arbitrary"),
        name="gla",
    )(proj, proj, proj, proj, small, wlr_pad, blr, gain, tril)


FF_LANE0 = GLA_RANK
C_LANE0 = HEAD_DIM
X_LANE0 = 96
SUB = 256
V_ROWS = HEAD_DIM + 16


def _fox_prep_kernel(q_ref, k_ref, v_ref, sm_ref, bf_ref, qg_ref, kg_ref, tril_ref, bd_ref,
                     sel_ref, cq_ref, ind_ref, qo_ref, ko_ref, vo_ref, co_ref, carry_ref, *, n_heads):
    @pl.when(pl.program_id(1) == 0)
    def _():
        carry_ref[...] = jnp.zeros_like(carry_ref)

    sm = sm_ref[0]
    tt = sm.shape[0]
    lane = lax.broadcasted_iota(jnp.int32, (tt, LANES), 1)
    is_ff = (lane >= FF_LANE0) & (lane < FF_LANE0 + n_heads)
    ls = jnp.where(is_ff, _log_sigmoid(sm + bf_ref[...]), 0.0)
    tril = tril_ref[...]
    p0, p1, p2 = _split3(ls)
    cum = carry_ref[0:1, :] + (_dot(tril, p0) + _dot(tril, p1) + _dot(tril, p2))
    carry_ref[0:1, :] = cum[tt - 1:tt, :]
    co_ref[0] = cum
    c0, c1, c2 = _split3(cum * LOG2E)
    e = _dot(c0, sel_ref[0]) + _dot(c1, sel_ref[1]) + _dot(c2, sel_ref[2])
    ex_q = e[:, :LANES] + cq_ref[...]
    ek_all = e[:, LANES:]
    bd = bd_ref[...]
    low = lane < HEAD_DIM
    is_x = (lane >= X_LANE0) & (lane < X_LANE0 + 3)

    def normed(x_ref, gain_ref, scale, pair):
        x = x_ref[0, :, pair * LANES:(pair + 1) * LANES].astype(F32)
        ms = _dot((x * x).astype(BF16), bd)
        return x * lax.rsqrt(ms + EPS) * (gain_ref[:, pair * LANES:(pair + 1) * LANES] * scale)

    for pair in range(n_heads // 2):
        qn = normed(q_ref, qg_ref, HEAD_DIM ** -0.5 * LOG2E, pair)
        kn = normed(k_ref, kg_ref, 1.0, pair)
        for odd in range(2):
            h = 2 * pair + odd
            if odd:
                qh = pltpu.roll(qn, HEAD_DIM, axis=1)
                kh = pltpu.roll(kn, HEAD_DIM, axis=1)
            else:
                qh, kh = qn, kn
            ex_k = jnp.where(is_x, pltpu.roll(ek_all, (X_LANE0 - 16 * h) % LANES, axis=1),
                             ind_ref[h:h + 1, :])
            qo_ref[0, h] = jnp.where(low, qh, ex_q).astype(BF16)
            ko_ref[0, h] = jnp.where(low, kh, ex_k).astype(BF16)

    vt = v_ref[0].astype(F32).T
    ones_rows = (lax.broadcasted_iota(jnp.int32, (V_ROWS - HEAD_DIM, SUB), 0) == 0).astype(BF16)
    for h in range(n_heads):
        for jb in range(tt // SUB):
            vo_ref[0, h, jb, :HEAD_DIM, :] = vt[h * HEAD_DIM:(h + 1) * HEAD_DIM,
                                                jb * SUB:(jb + 1) * SUB].astype(BF16)
            vo_ref[0, h, jb, HEAD_DIM:, :] = ones_rows


def _fox_prep_consts(n_heads):
    sel = np.zeros((3, LANES, 2 * LANES), np.float32)
    cq = np.zeros((1, LANES), np.float32)
    ind = np.zeros((n_heads, LANES), np.float32)
    for h in range(n_heads):
        for r in range(3):
            sel[r, FF_LANE0 + h, C_LANE0 + 3 * h + r] = 1.0
            sel[r, FF_LANE0 + h, LANES + 16 * h + r] = -1.0
            ind[h, C_LANE0 + 3 * h + r] = 1.0
            cq[0, X_LANE0 + r] = 1.0
    bd = np.kron(np.eye(LANES // HEAD_DIM, dtype=np.float32),
                 np.full((HEAD_DIM, HEAD_DIM), 1.0 / HEAD_DIM, np.float32))
    return jnp.asarray(sel, BF16), jnp.asarray(cq), jnp.asarray(ind), jnp.asarray(bd, BF16)


def _fox_prep(proj, small, bf_pad, q_gain, k_gain, q_col, k_col, v_col, tt):
    bsz, s, _ = proj.shape
    w = q_gain.shape[1]
    n_heads = w // HEAD_DIM
    sel, cq, ind, bd = _fox_prep_consts(n_heads)
    tril = jnp.asarray(np.tril(np.ones((tt, tt), np.float32)), BF16)
    out_spec = pl.BlockSpec((1, n_heads, tt, LANES), lambda b, i: (b, 0, i, 0))
    out_sds = jax.ShapeDtypeStruct((bsz, n_heads, s, LANES), BF16)

    def col(j):
        return pl.BlockSpec((1, tt, w), lambda b, i: (b, i, j))

    return pl.pallas_call(
        functools.partial(_fox_prep_kernel, n_heads=n_heads),
        grid=(bsz, s // tt),
        in_specs=[col(q_col), col(k_col), col(v_col),
                  pl.BlockSpec((1, tt, LANES), lambda b, i: (b, i, 0)),
                  _const_spec((1, LANES)), _const_spec((1, w)), _const_spec((1, w)),
                  _const_spec((tt, tt)), _const_spec((LANES, LANES)),
                  _const_spec(sel.shape), _const_spec(cq.shape), _const_spec(ind.shape)],
        out_specs=[out_spec, out_spec,
                   pl.BlockSpec((1, n_heads, tt // SUB, V_ROWS, SUB), lambda b, i: (b, 0, i, 0, 0)),
                   pl.BlockSpec((1, tt, LANES), lambda b, i: (b, i, 0))],
        out_shape=[out_sds, out_sds,
                   jax.ShapeDtypeStruct((bsz, n_heads, s // SUB, V_ROWS, SUB), BF16),
                   jax.ShapeDtypeStruct((bsz, s, LANES), F32)],
        scratch_shapes=[pltpu.VMEM((8, LANES), F32)],
        compiler_params=_cparams("parallel", "arbitrary"),
        name="fox_prep",
    )(proj, proj, proj, small, bf_pad, q_gain, k_gain, tril, bd, sel, cq, ind)


ZERO_MARGIN = 160.0


def _fox_attn_kernel(cq_ref, ce_ref, thr_ref, q_ref, k_ref, vt_ref, o_ref, m_ref, acc_ref, s_ref,
                     tri_ref, *, n_sub):
    b = pl.program_id(0)
    pair = pl.program_id(1)
    i = pl.program_id(2)
    row = lax.broadcasted_iota(jnp.int32, (SUB, SUB), 0)
    col = lax.broadcasted_iota(jnp.int32, (SUB, SUB), 1)
    tri_ref[...] = jnp.where(row <= col, 0.0, NEG_BIG)

    def blocks(r):
        a = i * n_sub + r
        return a, jnp.maximum(a - 1, 0)

    for r in range(n_sub):
        a, jp = blocks(r)
        for hh in range(2):
            q = q_ref[0, hh, r * SUB:(r + 1) * SUB, :]
            s_ref[r, hh, :SUB, :] = _dot_nt(k_ref[0, hh, pl.ds(pl.multiple_of(jp * SUB, SUB), SUB), :], q)
            s_ref[r, hh, SUB:, :] = _dot_nt(k_ref[0, hh, pl.ds(pl.multiple_of(a * SUB, SUB), SUB), :], q)
    for r in range(n_sub):
        a, jp = blocks(r)
        no_prev = jnp.where(a == 0, NEG_BIG, 0.0)
        for hh in range(2):
            st = s_ref[r, hh, :SUB, :] + no_prev
            sd = s_ref[r, hh, SUB:, :] + tri_ref[...]
            m = jnp.maximum(jnp.max(st, axis=0, keepdims=True), jnp.max(sd, axis=0, keepdims=True))
            pt = jnp.exp2((st - m).astype(BF16))
            pd = jnp.exp2((sd - m).astype(BF16))
            acc_ref[r, hh] = _dot(vt_ref[0, hh, jp], pt) + _dot(vt_ref[0, hh, a], pd)
            m_ref[r, hh, 0:1, :] = m

    for r in range(n_sub):
        a = i * n_sub + r
        outs = []
        for hh in range(2):
            h = 2 * pair + hh
            cq = cq_ref[b, h, a]
            thr = thr_ref[h]
            q = q_ref[0, hh, r * SUB:(r + 1) * SUB, :]

            def cond(c):
                j = c[0]
                return (j >= 0) & (cq - ce_ref[b, h, jnp.maximum(j, 0)] >= thr)

            def body(c):
                j, m, acc = c
                kb = k_ref[0, hh, pl.ds(pl.multiple_of(j * SUB, SUB), SUB), :]
                s = _dot_nt(kb, q)
                m_new = jnp.maximum(m, jnp.max(s, axis=0, keepdims=True))
                p = jnp.exp2((s - m_new).astype(BF16))
                return j - 1, m_new, jnp.exp2(m - m_new) * acc + _dot(vt_ref[0, hh, j], p)

            _, _, acc = lax.while_loop(cond, body, (a - 2, m_ref[r, hh, 0:1, :], acc_ref[r, hh]))
            outs.append(acc[:HEAD_DIM] / acc[HEAD_DIM:HEAD_DIM + 1])
        o_ref[0, r * SUB:(r + 1) * SUB, :] = jnp.concatenate(outs, axis=0).T.astype(BF16)


def _fox_attn(cq, ce, thr, qp, kp, vt, tq):
    bsz, n_heads, s, _ = qp.shape
    nkb = s // SUB
    n_sub = tq // SUB
    grid_spec = pltpu.PrefetchScalarGridSpec(
        num_scalar_prefetch=3,
        grid=(bsz, n_heads // 2, s // tq),
        in_specs=[pl.BlockSpec((1, 2, tq, LANES), lambda b, p, i, *_: (b, p, i, 0)),
                  pl.BlockSpec((1, 2, s, LANES), lambda b, p, i, *_: (b, p, 0, 0)),
                  pl.BlockSpec((1, 2, nkb, V_ROWS, SUB), lambda b, p, i, *_: (b, p, 0, 0, 0))],
        out_specs=pl.BlockSpec((1, tq, LANES), lambda b, p, i, *_: (b, i, p)),
        scratch_shapes=[pltpu.VMEM((n_sub, 2, 8, SUB), F32),
                        pltpu.VMEM((n_sub, 2, V_ROWS, SUB), F32),
                        pltpu.VMEM((n_sub, 2, 2 * SUB, SUB), F32),
                        pltpu.VMEM((SUB, SUB), F32)])
    return pl.pallas_call(
        functools.partial(_fox_attn_kernel, n_sub=n_sub),
        grid_spec=grid_spec,
        out_shape=jax.ShapeDtypeStruct((bsz, s, n_heads * HEAD_DIM), BF16),
        compiler_params=_cparams("parallel", "parallel", "arbitrary"),
        name="fox_attn",
    )(cq, ce, thr, qp, kp, vt)


def _fox_skip_tables(cum, q_gain, k_gain, n_heads):
    bsz, s, _ = cum.shape
    c2 = cum[:, :, FF_LANE0:FF_LANE0 + n_heads] * LOG2E
    cq = c2[:, ::SUB, :].transpose(0, 2, 1)
    ce = c2[:, SUB - 1::SUB, :].transpose(0, 2, 1)
    bound = (jnp.max(jnp.abs(q_gain), axis=-1) * jnp.max(jnp.abs(k_gain), axis=-1)
             * (HEAD_DIM ** 0.5) * LOG2E)
    thr = -(2.0 * 1.02 * bound + ZERO_MARGIN)
    return cq, ce, thr


def _s5_kernel(u_ref, kt_ref, min_ref, mout_ref, a_ref, d_ref, o_ref,
               t_ref, v_ref, vs_ref, xp_ref, *, bsz, n_col):
    L = S5_CHUNK
    W = S5_GROUP_WIDTH
    per = LANES // L
    nc = u_ref.shape[1]
    row = lax.broadcasted_iota(jnp.int32, (L, LANES), 0)
    lane = lax.broadcasted_iota(jnp.int32, (L, LANES), 1)
    causal = (lane % L) >= row

    def build(j, carry):
        kj = kt_ref[0, j]
        for ip in range(W // per):
            tile = jnp.broadcast_to(kj[ip:ip + 1, :], (L, LANES))
            tile = pltpu.roll(tile, 0, axis=1, stride=1, stride_axis=0)
            t_ref[pl.ds(pl.multiple_of(j * L, L), L), ip * LANES:(ip + 1) * LANES] = (
                jnp.where(causal, tile, 0.0).astype(BF16))
        return carry

    lax.fori_loop(0, W, build, 0)

    u = u_ref[0]
    v = _dot(u, min_ref[0])
    v_ref[...] = v
    vs_ref[...] = pltpu.roll(v, S5_STATE, axis=1)
    a1 = a_ref[0, 0:1, :]
    a2 = a_ref[0, 1:2, :]

    def scan(c, carry):
        x, xs = carry
        rows = pl.ds(pl.multiple_of(c * bsz, bsz), bsz)
        xp_ref[rows, :] = x
        return (x * a1 + xs * a2 + v_ref[rows, :], xs * a1 - x * a2 + vs_ref[rows, :])

    zero = jnp.zeros((bsz, 2 * S5_STATE), F32)
    lax.fori_loop(0, nc // bsz, scan, (zero, zero), unroll=8)

    xp = xp_ref[...].astype(BF16)
    for c0 in range(0, W * L, n_col):
        cs = slice(c0, c0 + n_col)
        y = (_dot(u, t_ref[:, cs]) + _dot(xp, mout_ref[0, :, cs])
             + d_ref[0, :, cs] * u[:, cs].astype(F32))
        o_ref[0, :, cs] = jax.nn.gelu(y).astype(BF16)


def _s5(u_g, ktab, m_in, m_out, a_pack, d_vec, bsz):
    groups, nc, wl = u_g.shape
    return pl.pallas_call(
        functools.partial(_s5_kernel, bsz=bsz, n_col=512),
        grid=(groups,),
        in_specs=[pl.BlockSpec((1, nc, wl), lambda g: (g, 0, 0)),
                  pl.BlockSpec((1,) + ktab.shape[1:], lambda g: (g, 0, 0, 0)),
                  pl.BlockSpec((1, wl, 2 * S5_STATE), lambda g: (g, 0, 0)),
                  pl.BlockSpec((1, 2 * S5_STATE, wl), lambda g: (g, 0, 0)),
                  pl.BlockSpec((1, 2, 2 * S5_STATE), lambda g: (g, 0, 0)),
                  pl.BlockSpec((1, 1, wl), lambda g: (g, 0, 0))],
        out_specs=pl.BlockSpec((1, nc, wl), lambda g: (g, 0, 0)),
        out_shape=jax.ShapeDtypeStruct((groups, nc, wl), BF16),
        scratch_shapes=[pltpu.VMEM((wl, wl), BF16),
                        pltpu.VMEM((nc, 2 * S5_STATE), F32),
                        pltpu.VMEM((nc, 2 * S5_STATE), F32),
                        pltpu.VMEM((nc, 2 * S5_STATE), F32)],
        compiler_params=_cparams("parallel"),
        name="s5",
    )(u_g, ktab, m_in, m_out, a_pack, d_vec)


def _s5_tables(lam_re, lam_im, log_dt, b_re, b_im, c_re, c_im, d_skip):
    L = S5_CHUNK
    dt = jnp.exp(log_dt)[:, None]
    mag = jnp.exp(lam_re * dt)
    ang = lam_im * dt
    ar, ai = mag * jnp.cos(ang), mag * jnp.sin(ang)
    den = lam_re * lam_re + lam_im * lam_im
    cr = ((ar - 1.0) * lam_re + ai * lam_im) / den
    ci = (ai * lam_re - (ar - 1.0) * lam_im) / den
    bbr = cr[..., None] * b_re - ci[..., None] * b_im
    bbi = cr[..., None] * b_im + ci[..., None] * b_re
    pr, pi = jnp.ones_like(ar)[None], jnp.zeros_like(ar)[None]
    sr, si = ar, ai
    n = 1
    while n <= L:
        pr, pi = (jnp.concatenate([pr, pr * sr - pi * si], 0),
                  jnp.concatenate([pi, pr * si + pi * sr], 0))
        sr, si = sr * sr - si * si, 2.0 * sr * si
        n *= 2
    pr, pi = pr[:L + 1], pi[:L + 1]
    hp = lax.Precision.HIGHEST
    er = c_re[None] * pr[:L, :, None, :] - c_im[None] * pi[:L, :, None, :]
    ei = -(c_re[None] * pi[:L, :, None, :] + c_im[None] * pr[:L, :, None, :])
    ktab = (jnp.einsum('lgip,gpj->gjil', er, bbr, precision=hp)
            + jnp.einsum('lgip,gpj->gjil', ei, bbi, precision=hp))
    qr, qi = pr[:L][::-1], pi[:L][::-1]
    min_re = qr[:, :, :, None] * bbr[None] - qi[:, :, :, None] * bbi[None]
    min_im = qr[:, :, :, None] * bbi[None] + qi[:, :, :, None] * bbr[None]
    m_in = jnp.concatenate([min_re, min_im], axis=2)
    m_in = m_in.transpose(1, 3, 0, 2).reshape(m_in.shape[1], -1, 2 * S5_STATE)
    tr, ti = pr[1:L + 1], pi[1:L + 1]
    mo_re = c_re[None] * tr[:, :, None, :] - c_im[None] * ti[:, :, None, :]
    mo_im = -(c_re[None] * ti[:, :, None, :] + c_im[None] * tr[:, :, None, :])
    m_out = jnp.concatenate([mo_re, mo_im], axis=3)
    m_out = m_out.transpose(1, 3, 2, 0).reshape(m_out.shape[1], 2 * S5_STATE, -1)
    a_pack = jnp.stack([jnp.concatenate([pr[L], pr[L]], -1),
                        jnp.concatenate([-pi[L], pi[L]], -1)], axis=1)
    d_vec = jnp.repeat(d_skip, L, axis=-1)[:, None, :]
    groups, width = ktab.shape[:2]
    ktab = ktab.reshape(groups, width, width * L // LANES, LANES)
    return ktab, m_in.astype(BF16), m_out.astype(BF16), a_pack, d_vec


def _sgu_kernel(u_ref, v_ref, w_ref, bias_ref, lg_ref, lb_ref, o_ref, *, n_chunks):
    L = SGU_CHUNK
    row = lax.broadcasted_iota(jnp.int32, (L, L), 0)
    colm = lax.broadcasted_iota(jnp.int32, (L, L), 1)
    causal = colm <= row
    v = jax.nn.gelu(v_ref[0].astype(F32))
    mu = jnp.mean(v, axis=-1, keepdims=True)
    vc = v - mu
    var = jnp.mean(vc * vc, axis=-1, keepdims=True)
    vn = (vc * lax.rsqrt(var + EPS) * lg_ref[...] + lb_ref[...]).astype(BF16)
    gw = vn.shape[1] // SGU_GROUPS
    ws = [jnp.where(causal, w_ref[g], 0.0).astype(BF16) for g in range(SGU_GROUPS)]
    for c in range(n_chunks):
        rows = slice(c * L, (c + 1) * L)
        mixed = jnp.concatenate(
            [_dot(ws[g], vn[rows, g * gw:(g + 1) * gw]) for g in range(SGU_GROUPS)], axis=-1)
        u = jax.nn.gelu(u_ref[0, rows, :].astype(F32))
        o_ref[0, rows, :] = (u * (mixed + bias_ref[...])).astype(BF16)


def _sgu(proj, w_s, bias_full, ln_gain, ln_bias, u_col, v_col, tt):
    bsz, s, _ = proj.shape
    w = ln_gain.shape[1]
    return pl.pallas_call(
        functools.partial(_sgu_kernel, n_chunks=tt // SGU_CHUNK),
        grid=(bsz, s // tt),
        in_specs=[pl.BlockSpec((1, tt, w), lambda b, i: (b, i, u_col)),
                  pl.BlockSpec((1, tt, w), lambda b, i: (b, i, v_col)),
                  _const_spec(w_s.shape), _const_spec(bias_full.shape),
                  _const_spec((1, w)), _const_spec((1, w))],
        out_specs=pl.BlockSpec((1, tt, w), lambda b, i: (b, i, 0)),
        out_shape=jax.ShapeDtypeStruct((bsz, s, w), BF16),
        compiler_params=_cparams("parallel", "parallel"),
        name="sgu",
    )(proj, proj, w_s, bias_full, ln_gain, ln_bias)


def _tail_kernel(x_ref, a_ref, b_ref, wo_ref, g1_ref, sc_ref, sh_ref, g2_ref, w1_ref, w2_ref,
                 *rest, f_chunk, with_glu):
    if with_glu:
        wg_ref, bg_ref, o_ref = rest
    else:
        (o_ref,) = rest
    a = a_ref[0]
    half = a.shape[1]
    if with_glu:
        af = a.astype(F32)
        a = (af * jax.nn.sigmoid(_dot(a, wg_ref[...]) + bg_ref[...])).astype(BF16)
    y = _dot(a, wo_ref[:half, :]) + _dot(b_ref[0], wo_ref[half:, :])
    x1 = x_ref[0] + g1_ref[0] * y
    h = _norm_mod(x1, sc_ref[0], sh_ref[0]).astype(BF16)
    f = w1_ref.shape[1]
    acc = None
    for f0 in range(0, f, f_chunk):
        hid = jnp.maximum(_dot(h, w1_ref[:, f0:f0 + f_chunk]), 0.0)
        part = _dot((hid * hid).astype(BF16), w2_ref[f0:f0 + f_chunk, :])
        acc = part if acc is None else acc + part
    o_ref[0] = x1 + g2_ref[0] * acc


def _tail(x, mix_a, mix_b, w_out, g1, sc, sh, g2, w1, w2, glu, tm):
    bsz, s, d = x.shape
    half = mix_a.shape[-1]
    f = w1.shape[1]
    row = pl.BlockSpec((1, 1, d), lambda b, i: (b, 0, 0))
    in_specs = [pl.BlockSpec((1, tm, d), lambda b, i: (b, i, 0)),
                pl.BlockSpec((1, tm, half), lambda b, i: (b, i, 0)),
                pl.BlockSpec((1, tm, half), lambda b, i: (b, i, 0)),
                _const_spec((d, d)), row, row, row, row,
                _const_spec((d, f)), _const_spec((f, d))]
    args = [x, mix_a, mix_b, w_out, g1, sc, sh, g2, w1, w2]
    if glu is not None:
        in_specs += [_const_spec((half, half)), _const_spec((1, half))]
        args += list(glu)
    return pl.pallas_call(
        functools.partial(_tail_kernel, f_chunk=1024, with_glu=glu is not None),
        grid=(bsz, s // tm),
        in_specs=in_specs,
        out_specs=pl.BlockSpec((1, tm, d), lambda b, i: (b, i, 0)),
        out_shape=jax.ShapeDtypeStruct((bsz, s, d), F32),
        compiler_params=_cparams("parallel", "parallel"),
        name="tail",
    )(*args)


def _tile(s, want):
    t = min(s, want)
    assert s % t == 0, (s, t)
    return t


def kernel(x, c, ada_w, ada_b, even_w_in, even_w_out, gla_w_lr, gla_b_lr, gla_gain, fox_b_f, fox_q_gain, fox_k_gain, odd_w_in, odd_w_out, s5_lam_re, s5_lam_im, s5_log_dt, s5_b_re, s5_b_im, s5_c_re, s5_c_im, s5_d, s5_w_glu, s5_b_glu, sgu_ln_gain, sgu_ln_bias, sgu_w_s, sgu_b_s, mlp_w1, mlp_w2):
    bsz, s, d = x.shape
    half = d // 2
    n_heads = half // HEAD_DIM
    tm = _tile(s, 512)

    mod = _ada_mod(c, ada_w, ada_b)

    def mods(layer):
        return [m[:, None, :] for m in jnp.split(mod[layer], 6, axis=-1)]

    sh1, sc1, g1, sh2, sc2, g2 = mods(0)
    w_in = even_w_in[0]
    o_lr = 4 * half
    o_fq = o_lr + GLA_RANK
    o_ff = o_fq + 3 * half
    w_main = jnp.concatenate([w_in[:, :o_lr], w_in[:, o_fq:o_ff]], axis=1).astype(BF16)
    w_small = jnp.concatenate(
        [w_in[:, o_lr:o_fq], w_in[:, o_ff:],
         jnp.zeros((d, LANES - GLA_RANK - n_heads), w_in.dtype)], axis=1).astype(BF16)
    proj, small = _inproj(x, sc1, sh1, w_main, w_small, tm)

    wlr_pad = jnp.concatenate([gla_w_lr[0], jnp.zeros((LANES - GLA_RANK, half), F32)], axis=0)
    o_gla = _gla(proj, small, wlr_pad, gla_b_lr[0][None, :], gla_gain[0].reshape(1, half), tm)

    bf_pad = jnp.zeros((1, LANES), F32).at[0, FF_LANE0:FF_LANE0 + n_heads].set(fox_b_f[0])
    qp, kp, vt, cum = _fox_prep(proj, small, bf_pad, fox_q_gain[0].reshape(1, half),
                                fox_k_gain[0].reshape(1, half), 4, 5, 6, tm)
    cq, ce, thr = _fox_skip_tables(cum, fox_q_gain[0], fox_k_gain[0], n_heads)
    o_fox = _fox_attn(cq, ce, thr, qp, kp, vt, _tile(s, 1024))

    x = _tail(x, o_gla, o_fox, even_w_out[0].astype(BF16), g1, sc2, sh2, g2,
              mlp_w1[0].astype(BF16), mlp_w2[0].astype(BF16), None, tm)

    sh1, sc1, g1, sh2, sc2, g2 = mods(1)
    (proj,) = _inproj(x, sc1, sh1, odd_w_in[0].astype(BF16), None, tm)

    groups = half // S5_GROUP_WIDTH
    nck = s // S5_CHUNK
    u = proj[:, :, :half].reshape(bsz, nck, S5_CHUNK, groups, S5_GROUP_WIDTH)
    u_g = u.transpose(3, 1, 0, 4, 2).reshape(groups, nck * bsz, S5_GROUP_WIDTH * S5_CHUNK)
    ktab, m_in, m_out, a_pack, d_vec = _s5_tables(
        s5_lam_re[0], s5_lam_im[0], s5_log_dt[0], s5_b_re[0], s5_b_im[0],
        s5_c_re[0], s5_c_im[0], s5_d[0])
    y_g = _s5(u_g, ktab, m_in, m_out, a_pack, d_vec, bsz)
    y = y_g.reshape(groups, nck, bsz, S5_GROUP_WIDTH, S5_CHUNK).transpose(2, 1, 4, 0, 3)
    y = y.reshape(bsz, s, half)

    bias_full = jnp.repeat(sgu_b_s[0].T, half // SGU_GROUPS, axis=1)
    y_sgu = _sgu(proj, sgu_w_s[0], bias_full, sgu_ln_gain[0][None, :], sgu_ln_bias[0][None, :],
                 1, 2, tm)

    x = _tail(x, y, y_sgu, odd_w_out[0].astype(BF16), g1, sc2, sh2, g2,
              mlp_w1[1].astype(BF16), mlp_w2[1].astype(BF16),
              (s5_w_glu[0].astype(BF16), s5_b_glu[0][None, :]), tm)
    return x
```

```python
import functools

import numpy as np
import jax
import jax.numpy as jnp
from jax import lax
from jax.experimental import pallas as pl
from jax.experimental.pallas import tpu as pltpu

F32 = jnp.float32
BF16 = jnp.bfloat16

EPS = 1e-6
HEAD_DIM = 64
LANES = 128
GLA_RANK = 16
GLA_TAU = 16.0
GLA_CHUNK = 64
S5_GROUP_WIDTH = 16
S5_STATE = 64
S5_CHUNK = 128
SGU_GROUPS = 8
SGU_CHUNK = 128
VMEM_LIMIT_BYTES = 56 * 1024 * 1024
NEG_BIG = -1e30
LOG2E = 1.4426950408889634


def _cparams(*sem):
    return pltpu.CompilerParams(dimension_semantics=sem, vmem_limit_bytes=VMEM_LIMIT_BYTES)


def _const_spec(shape):
    return pl.BlockSpec(shape, lambda *_: (0,) * len(shape), pipeline_mode=pl.Buffered(1))


def _dot(a, b):
    return jnp.dot(a, b, preferred_element_type=F32)


def _dot_nt(a, b):
    return lax.dot_general(a, b, (((1,), (1,)), ((), ())), preferred_element_type=F32)


def _dot_tn(a, b):
    return lax.dot_general(a, b, (((0,), (0,)), ((), ())), preferred_element_type=F32)


def _split2(x):
    hi = x.astype(BF16)
    lo = (x - hi.astype(F32)).astype(BF16)
    return hi, lo


def _split3(x):
    hi = x.astype(BF16)
    r = x - hi.astype(F32)
    mid = r.astype(BF16)
    lo = (r - mid.astype(F32)).astype(BF16)
    return hi, mid, lo


def _log_sigmoid(z):
    return jnp.minimum(z, 0.0) - jnp.log(1.0 + jnp.exp(-jnp.abs(z)))


def _head_mean_matrix():
    return jnp.asarray(np.kron(np.eye(LANES // HEAD_DIM, dtype=np.float32),
                               np.full((HEAD_DIM, HEAD_DIM), 1.0 / HEAD_DIM, np.float32)), BF16)


def _norm_mod(x, sc, sh):
    ms = jnp.mean(x * x, axis=-1, keepdims=True)
    return x * lax.rsqrt(ms + EPS) * (1.0 + sc) + sh


def _ada_kernel(c_ref, w_ref, b_ref, o_ref):
    c = c_ref[...]
    ca = c * jax.nn.sigmoid(c)
    o_ref[0] = _dot(ca, w_ref[0]) + b_ref[0]


def _ada_mod(c, ada_w, ada_b):
    depth, d, n = ada_w.shape
    bsz = c.shape[0]
    tn = n // 4
    return pl.pallas_call(
        _ada_kernel,
        grid=(depth, n // tn),
        in_specs=[pl.BlockSpec((bsz, d), lambda l, j: (0, 0)),
                  pl.BlockSpec((1, d, tn), lambda l, j: (l, 0, j)),
                  pl.BlockSpec((1, 1, tn), lambda l, j: (l, 0, j))],
        out_specs=pl.BlockSpec((1, bsz, tn), lambda l, j: (l, 0, j)),
        out_shape=jax.ShapeDtypeStruct((depth, bsz, n), F32),
        compiler_params=_cparams("parallel", "parallel"),
        name="ada_mod",
    )(c, ada_w, ada_b.reshape(depth, 1, n))


def _inproj_kernel(x_ref, sc_ref, sh_ref, *rest, widths, n_chunk, with_small, with_t):
    rest = list(rest)
    wm_ref = rest.pop(0)
    ws_ref = rest.pop(0) if with_small else None
    wt_ref = rest.pop(0) if with_t else None
    h = _norm_mod(x_ref[0], sc_ref[0], sh_ref[0]).astype(BF16)
    base = 0
    for width in widths:
        o_ref = rest.pop(0)
        for c0 in range(0, width, n_chunk):
            o_ref[0, :, c0:c0 + n_chunk] = _dot(
                h, wm_ref[:, base + c0:base + c0 + n_chunk]).astype(BF16)
        base += width
    if with_small:
        rest.pop(0)[0] = _dot(h, ws_ref[...])
    if with_t:
        rest.pop(0)[...] = _dot_nt(wt_ref[...], h).astype(BF16)


def _inproj(x, sc, sh, w_main, widths, tm, w_small=None, w_t=None):
    bsz, s, d = x.shape
    n = w_main.shape[1]
    assert sum(widths) == n
    nt = s // tm
    in_specs = [pl.BlockSpec((1, tm, d), lambda b, i: (b, i, 0)),
                pl.BlockSpec((1, 1, d), lambda b, i: (b, 0, 0)),
                pl.BlockSpec((1, 1, d), lambda b, i: (b, 0, 0)),
                _const_spec((d, n))]
    args = [x, sc, sh, w_main]
    out_specs = [pl.BlockSpec((1, tm, wd), lambda b, i: (b, i, 0)) for wd in widths]
    out_shape = [jax.ShapeDtypeStruct((bsz, s, wd), BF16) for wd in widths]
    if w_small is not None:
        in_specs.append(_const_spec((d, LANES)))
        args.append(w_small)
        out_specs.append(pl.BlockSpec((1, tm, LANES), lambda b, i: (b, i, 0)))
        out_shape.append(jax.ShapeDtypeStruct((bsz, s, LANES), F32))
    if w_t is not None:
        ch = w_t.shape[0]
        in_specs.append(_const_spec((ch, d)))
        args.append(w_t)
        out_specs.append(pl.BlockSpec((ch, tm), lambda b, i: (0, b * nt + i)))
        out_shape.append(jax.ShapeDtypeStruct((ch, bsz * s), BF16))
    return pl.pallas_call(
        functools.partial(_inproj_kernel, widths=tuple(widths), n_chunk=512,
                          with_small=w_small is not None, with_t=w_t is not None),
        grid=(bsz, nt),
        in_specs=in_specs, out_specs=out_specs, out_shape=out_shape,
        compiler_params=_cparams("parallel", "parallel"),
        name="inproj",
    )(*args)


def _gla_kernel(q_ref, k_ref, v_ref, g_ref, sm_ref, wlr_ref, blr_ref, gain_ref, tril_ref, bd_ref,
                o_ref, state_ref, a_ref, kv_ref, st_ref, qin_ref, ebl_ref, o_acc_ref,
                *, n_heads, n_chunks):
    C = GLA_CHUNK
    mid = C // 2 - 1

    @pl.when(pl.program_id(1) == 0)
    def _():
        state_ref[...] = jnp.zeros_like(state_ref)

    row = lax.broadcasted_iota(jnp.int32, (C, C), 0)
    col = lax.broadcasted_iota(jnp.int32, (C, C), 1)
    causal = col <= row
    z = _dot(sm_ref[0].astype(BF16), wlr_ref[...]) + blr_ref[...]
    lc_hi, lc_lo = _split2(_log_sigmoid(z) * (1.0 / GLA_TAU))
    tril = tril_ref[...]
    bc_all = _dot(tril, lc_hi) + _dot(tril, lc_lo)
    heads = [slice(h * HEAD_DIM, (h + 1) * HEAD_DIM) for h in range(n_heads)]

    for c in range(n_chunks):
        rows = slice(c * C, (c + 1) * C)
        bc = bc_all[rows]
        m = bc[mid:mid + 1, :]
        bl = bc[C - 1:C, :]
        e = jnp.exp(bc - m)
        qt = q_ref[0, rows, :].astype(F32) * (HEAD_DIM ** -0.5) * e
        kt = k_ref[0, rows, :].astype(F32) * (1.0 / e)
        qin_ref[rows, :] = (qt * jnp.exp(m)).astype(BF16)
        kst = (kt * jnp.exp(bl - m)).astype(BF16)
        qt = qt.astype(BF16)
        kt = kt.astype(BF16)
        ebl_ref[c:c + 1, :] = jnp.exp(bl)
        v = v_ref[0, rows, :]
        for h, sl in enumerate(heads):
            a_ref[c, h] = jnp.where(causal, _dot_nt(qt[:, sl], kt[:, sl]), 0.0).astype(BF16)
            kv_ref[c, h] = _dot_tn(v[:, sl], kst[:, sl])

    for h, sl in enumerate(heads):
        st = state_ref[h]
        for c in range(n_chunks):
            st_ref[c, h] = st.astype(BF16)
            st = st * ebl_ref[c:c + 1, sl] + kv_ref[c, h]
        state_ref[h] = st

    for c in range(n_chunks):
        rows = slice(c * C, (c + 1) * C)
        for h, sl in enumerate(heads):
            o_acc_ref[rows, sl] = (_dot(a_ref[c, h], v_ref[0, rows, sl])
                                   + _dot_nt(qin_ref[rows, sl], st_ref[c, h]))

    bd = bd_ref[...]
    for p in range(n_heads * HEAD_DIM // LANES):
        cols = slice(p * LANES, (p + 1) * LANES)
        o = o_acc_ref[:, cols]
        ms = _dot((o * o).astype(BF16), bd)
        g = g_ref[0, :, cols].astype(F32)
        o_ref[0, :, cols] = (o * lax.rsqrt(ms + EPS) * gain_ref[:, cols]
                             * (g * jax.nn.sigmoid(g))).astype(BF16)


def _gla(proj, small, wlr_pad, blr, gain, tt):
    bsz, s, _ = proj.shape
    w = wlr_pad.shape[1]
    n_heads = w // HEAD_DIM

    def col(j):
        return pl.BlockSpec((1, tt, w), lambda b, i, j=j: (b, i, j))

    n_chunks = tt // GLA_CHUNK
    tril = jnp.asarray(np.kron(np.eye(n_chunks, dtype=np.float32),
                               np.tril(np.ones((GLA_CHUNK, GLA_CHUNK), np.float32))), BF16)
    bd = _head_mean_matrix()
    return pl.pallas_call(
        functools.partial(_gla_kernel, n_heads=n_heads, n_chunks=n_chunks),
        grid=(bsz, s // tt),
        in_specs=[col(0), col(1), col(2), col(3),
                  pl.BlockSpec((1, tt, LANES), lambda b, i: (b, i, 0)),
                  _const_spec((LANES, w)), _const_spec((1, w)), _const_spec((1, w)),
                  _const_spec((tt, tt)), _const_spec((LANES, LANES))],
        out_specs=pl.BlockSpec((1, tt, w), lambda b, i: (b, i, 0)),
        out_shape=jax.ShapeDtypeStruct((bsz, s, w), BF16),
        scratch_shapes=[pltpu.VMEM((n_heads, HEAD_DIM, HEAD_DIM), F32),
                        pltpu.VMEM((n_chunks, n_heads, GLA_CHUNK, GLA_CHUNK), BF16),
                        pltpu.VMEM((n_chunks, n_heads, HEAD_DIM, HEAD_DIM), F32),
                        pltpu.VMEM((n_chunks, n_heads, HEAD_DIM, HEAD_DIM), BF16),
                        pltpu.VMEM((tt, w), BF16),
                        pltpu.VMEM((n_chunks, w), F32),
                        pltpu.VMEM((tt, w), F32)],
        compiler_params=_cparams("parallel", "arbitrary"),
        name="gla",
    )(proj, proj, proj, proj, small, wlr_pad, blr, gain, tril, bd)


FF_LANE0 = GLA_RANK
C_LANE0 = HEAD_DIM
X_LANE0 = 96
SUB = 256
V_ROWS = HEAD_DIM + 16


def _fox_prep_kernel(q_ref, k_ref, v_ref, sm_ref, bf_ref, qg_ref, kg_ref, tril_ref, bd_ref,
                     sel_ref, cq_ref, ind_ref, qo_ref, ko_ref, vo_ref, co_ref, carry_ref, *, n_heads):
    @pl.when(pl.program_id(1) == 0)
    def _():
        carry_ref[...] = jnp.zeros_like(carry_ref)

    sm = sm_ref[0]
    tt = sm.shape[0]
    lane = lax.broadcasted_iota(jnp.int32, (tt, LANES), 1)
    is_ff = (lane >= FF_LANE0) & (lane < FF_LANE0 + n_heads)
    ls = jnp.where(is_ff, _log_sigmoid(sm + bf_ref[...]), 0.0)
    tril = tril_ref[...]
    p0, p1, p2 = _split3(ls)
    cum = carry_ref[0:1, :] + (_dot(tril, p0) + _dot(tril, p1) + _dot(tril, p2))
    carry_ref[0:1, :] = cum[tt - 1:tt, :]
    co_ref[0] = cum
    c0, c1, c2 = _split3(cum * LOG2E)
    e = _dot(c0, sel_ref[0]) + _dot(c1, sel_ref[1]) + _dot(c2, sel_ref[2])
    ex_q = e[:, :LANES] + cq_ref[...]
    ek_all = e[:, LANES:]
    bd = bd_ref[...]
    low = lane < HEAD_DIM
    is_x = (lane >= X_LANE0) & (lane < X_LANE0 + 3)

    def normed(x_ref, gain_ref, scale, pair):
        x = x_ref[0, :, pair * LANES:(pair + 1) * LANES].astype(F32)
        ms = _dot((x * x).astype(BF16), bd)
        return x * lax.rsqrt(ms + EPS) * (gain_ref[:, pair * LANES:(pair + 1) * LANES] * scale)

    for pair in range(n_heads // 2):
        qn = normed(q_ref, qg_ref, HEAD_DIM ** -0.5 * LOG2E, pair)
        kn = normed(k_ref, kg_ref, 1.0, pair)
        for odd in range(2):
            h = 2 * pair + odd
            if odd:
                qh = pltpu.roll(qn, HEAD_DIM, axis=1)
                kh = pltpu.roll(kn, HEAD_DIM, axis=1)
            else:
                qh, kh = qn, kn
            ex_k = jnp.where(is_x, pltpu.roll(ek_all, (X_LANE0 - 16 * h) % LANES, axis=1),
                             ind_ref[h:h + 1, :])
            qo_ref[0, h] = jnp.where(low, qh, ex_q).astype(BF16)
            ko_ref[0, h] = jnp.where(low, kh, ex_k).astype(BF16)

    vt = v_ref[0].astype(F32).T
    ones_rows = (lax.broadcasted_iota(jnp.int32, (V_ROWS - HEAD_DIM, SUB), 0) == 0).astype(BF16)
    for h in range(n_heads):
        for jb in range(tt // SUB):
            vo_ref[0, h, jb, :HEAD_DIM, :] = vt[h * HEAD_DIM:(h + 1) * HEAD_DIM,
                                                jb * SUB:(jb + 1) * SUB].astype(BF16)
            vo_ref[0, h, jb, HEAD_DIM:, :] = ones_rows


def _fox_prep_consts(n_heads):
    sel = np.zeros((3, LANES, 2 * LANES), np.float32)
    cq = np.zeros((1, LANES), np.float32)
    ind = np.zeros((n_heads, LANES), np.float32)
    for h in range(n_heads):
        for r in range(3):
            sel[r, FF_LANE0 + h, C_LANE0 + 3 * h + r] = 1.0
            sel[r, FF_LANE0 + h, LANES + 16 * h + r] = -1.0
            ind[h, C_LANE0 + 3 * h + r] = 1.0
            cq[0, X_LANE0 + r] = 1.0
    return jnp.asarray(sel, BF16), jnp.asarray(cq), jnp.asarray(ind), _head_mean_matrix()


def _fox_prep(proj, small, bf_pad, q_gain, k_gain, q_col, k_col, v_col, tt):
    bsz, s, _ = proj.shape
    w = q_gain.shape[1]
    n_heads = w // HEAD_DIM
    sel, cq, ind, bd = _fox_prep_consts(n_heads)
    tril = jnp.asarray(np.tril(np.ones((tt, tt), np.float32)), BF16)
    out_spec = pl.BlockSpec((1, n_heads, tt, LANES), lambda b, i: (b, 0, i, 0))
    out_sds = jax.ShapeDtypeStruct((bsz, n_heads, s, LANES), BF16)

    def col(j):
        return pl.BlockSpec((1, tt, w), lambda b, i: (b, i, j))

    return pl.pallas_call(
        functools.partial(_fox_prep_kernel, n_heads=n_heads),
        grid=(bsz, s // tt),
        in_specs=[col(q_col), col(k_col), col(v_col),
                  pl.BlockSpec((1, tt, LANES), lambda b, i: (b, i, 0)),
                  _const_spec((1, LANES)), _const_spec((1, w)), _const_spec((1, w)),
                  _const_spec((tt, tt)), _const_spec((LANES, LANES)),
                  _const_spec(sel.shape), _const_spec(cq.shape), _const_spec(ind.shape)],
        out_specs=[out_spec, out_spec,
                   pl.BlockSpec((1, n_heads, tt // SUB, V_ROWS, SUB), lambda b, i: (b, 0, i, 0, 0)),
                   pl.BlockSpec((1, tt, LANES), lambda b, i: (b, i, 0))],
        out_shape=[out_sds, out_sds,
                   jax.ShapeDtypeStruct((bsz, n_heads, s // SUB, V_ROWS, SUB), BF16),
                   jax.ShapeDtypeStruct((bsz, s, LANES), F32)],
        scratch_shapes=[pltpu.VMEM((8, LANES), F32)],
        compiler_params=_cparams("parallel", "arbitrary"),
        name="fox_prep",
    )(proj, proj, proj, small, bf_pad, q_gain, k_gain, tril, bd, sel, cq, ind)


ZERO_MARGIN = 160.0


def _fox_attn_kernel(cq_ref, ce_ref, thr_ref, q_ref, k_ref, vt_ref, o_ref, m_ref, acc_ref, s_ref,
                     tri_ref, *, n_sub):
    b = pl.program_id(0)
    pair = pl.program_id(1)
    i = pl.program_id(2)
    row = lax.broadcasted_iota(jnp.int32, (SUB, SUB), 0)
    col = lax.broadcasted_iota(jnp.int32, (SUB, SUB), 1)
    tri_ref[...] = jnp.where(row <= col, 0.0, NEG_BIG)

    def blocks(r):
        a = i * n_sub + r
        return a, jnp.maximum(a - 1, 0)

    for r in range(n_sub):
        a, jp = blocks(r)
        for hh in range(2):
            q = q_ref[0, hh, r * SUB:(r + 1) * SUB, :]
            s_ref[r, hh, :SUB, :] = _dot_nt(k_ref[0, hh, pl.ds(pl.multiple_of(jp * SUB, SUB), SUB), :], q)
            s_ref[r, hh, SUB:, :] = _dot_nt(k_ref[0, hh, pl.ds(pl.multiple_of(a * SUB, SUB), SUB), :], q)
    for r in range(n_sub):
        a, jp = blocks(r)
        no_prev = jnp.where(a == 0, NEG_BIG, 0.0)
        for hh in range(2):
            st = s_ref[r, hh, :SUB, :] + no_prev
            sd = s_ref[r, hh, SUB:, :] + tri_ref[...]
            m = jnp.maximum(jnp.max(st, axis=0, keepdims=True), jnp.max(sd, axis=0, keepdims=True))
            pt = jnp.exp2((st - m).astype(BF16))
            pd = jnp.exp2((sd - m).astype(BF16))
            acc_ref[r, hh] = _dot(vt_ref[0, hh, jp], pt) + _dot(vt_ref[0, hh, a], pd)
            m_ref[r, hh, 0:1, :] = m

    for r in range(n_sub):
        a = i * n_sub + r
        outs = []
        for hh in range(2):
            h = 2 * pair + hh
            cq = cq_ref[b, h, a]
            thr = thr_ref[h]
            q = q_ref[0, hh, r * SUB:(r + 1) * SUB, :]

            def cond(c):
                j = c[0]
                return (j >= 0) & (cq - ce_ref[b, h, jnp.maximum(j, 0)] >= thr)

            def body(c):
                j, m, acc = c
                kb = k_ref[0, hh, pl.ds(pl.multiple_of(j * SUB, SUB), SUB), :]
                s = _dot_nt(kb, q)
                m_new = jnp.maximum(m, jnp.max(s, axis=0, keepdims=True))
                p = jnp.exp2((s - m_new).astype(BF16))
                return j - 1, m_new, jnp.exp2(m - m_new) * acc + _dot(vt_ref[0, hh, j], p)

            _, _, acc = lax.while_loop(cond, body, (a - 2, m_ref[r, hh, 0:1, :], acc_ref[r, hh]))
            outs.append(acc[:HEAD_DIM] / acc[HEAD_DIM:HEAD_DIM + 1])
        o_ref[0, r * SUB:(r + 1) * SUB, :] = jnp.concatenate(outs, axis=0).T.astype(BF16)


def _fox_attn(cq, ce, thr, qp, kp, vt, tq):
    bsz, n_heads, s, _ = qp.shape
    nkb = s // SUB
    n_sub = tq // SUB
    grid_spec = pltpu.PrefetchScalarGridSpec(
        num_scalar_prefetch=3,
        grid=(bsz, n_heads // 2, s // tq),
        in_specs=[pl.BlockSpec((1, 2, tq, LANES), lambda b, p, i, *_: (b, p, i, 0)),
                  pl.BlockSpec((1, 2, s, LANES), lambda b, p, i, *_: (b, p, 0, 0)),
                  pl.BlockSpec((1, 2, nkb, V_ROWS, SUB), lambda b, p, i, *_: (b, p, 0, 0, 0))],
        out_specs=pl.BlockSpec((1, tq, LANES), lambda b, p, i, *_: (b, i, p)),
        scratch_shapes=[pltpu.VMEM((n_sub, 2, 8, SUB), F32),
                        pltpu.VMEM((n_sub, 2, V_ROWS, SUB), F32),
                        pltpu.VMEM((n_sub, 2, 2 * SUB, SUB), F32),
                        pltpu.VMEM((SUB, SUB), F32)])
    return pl.pallas_call(
        functools.partial(_fox_attn_kernel, n_sub=n_sub),
        grid_spec=grid_spec,
        out_shape=jax.ShapeDtypeStruct((bsz, s, n_heads * HEAD_DIM), BF16),
        compiler_params=_cparams("parallel", "parallel", "arbitrary"),
        name="fox_attn",
    )(cq, ce, thr, qp, kp, vt)


def _fox_skip_tables(cum, q_gain, k_gain, n_heads):
    bsz, s, _ = cum.shape
    c2 = cum[:, :, FF_LANE0:FF_LANE0 + n_heads] * LOG2E
    cq = c2[:, ::SUB, :].transpose(0, 2, 1)
    ce = c2[:, SUB - 1::SUB, :].transpose(0, 2, 1)
    bound = (jnp.max(jnp.abs(q_gain), axis=-1) * jnp.max(jnp.abs(k_gain), axis=-1)
             * (HEAD_DIM ** 0.5) * LOG2E)
    thr = -(2.0 * 1.02 * bound + ZERO_MARGIN)
    return cq, ce, thr


def _s5_kernel(u_ref, kt_ref, min_ref, mout_ref, a_ref, d_ref, o_ref,
               t_ref, v_ref, vs_ref, xp_ref, *, bsz, n_col):
    L = S5_CHUNK
    W = S5_GROUP_WIDTH
    per = LANES // L
    nc = u_ref.shape[1]
    row = lax.broadcasted_iota(jnp.int32, (L, LANES), 0)
    lane = lax.broadcasted_iota(jnp.int32, (L, LANES), 1)
    causal = (lane % L) >= row

    def build(j, carry):
        kj = kt_ref[0, j]
        for ip in range(W // per):
            tile = jnp.broadcast_to(kj[ip:ip + 1, :], (L, LANES))
            tile = pltpu.roll(tile, 0, axis=1, stride=1, stride_axis=0)
            t_ref[pl.ds(pl.multiple_of(j * L, L), L), ip * LANES:(ip + 1) * LANES] = (
                jnp.where(causal, tile, 0.0).astype(BF16))
        return carry

    lax.fori_loop(0, W, build, 0)

    u = jnp.concatenate([u_ref[j] for j in range(W)], axis=-1)
    v = _dot(u, min_ref[0])
    v_ref[...] = v
    vs_ref[...] = pltpu.roll(v, S5_STATE, axis=1)
    a1 = a_ref[0, 0:1, :]
    a2 = a_ref[0, 1:2, :]

    n_steps = nc // bsz

    def scan(c, carry):
        x, xs = carry
        rows = pl.ds(c, bsz, stride=n_steps)
        xp_ref[rows, :] = x
        return (x * a1 + xs * a2 + v_ref[rows, :], xs * a1 - x * a2 + vs_ref[rows, :])

    zero = jnp.zeros((bsz, 2 * S5_STATE), F32)
    lax.fori_loop(0, n_steps, scan, (zero, zero), unroll=8)

    xp = xp_ref[...].astype(BF16)
    for c0 in range(0, W * L, n_col):
        cs = slice(c0, c0 + n_col)
        y = (_dot(u, t_ref[:, cs]) + _dot(xp, mout_ref[0, :, cs])
             + d_ref[0, :, cs] * u[:, cs].astype(F32))
        y = jax.nn.gelu(y).astype(BF16)
        for k in range(n_col // L):
            o_ref[c0 // L + k] = y[:, k * L:(k + 1) * L]


def _s5(u_t, ktab, m_in, m_out, a_pack, d_vec, bsz):
    chans, nc, L = u_t.shape
    W = S5_GROUP_WIDTH
    wl = W * L
    return pl.pallas_call(
        functools.partial(_s5_kernel, bsz=bsz, n_col=512),
        grid=(chans // W,),
        in_specs=[pl.BlockSpec((W, nc, L), lambda g: (g, 0, 0)),
                  pl.BlockSpec((1,) + ktab.shape[1:], lambda g: (g, 0, 0, 0)),
                  pl.BlockSpec((1, wl, 2 * S5_STATE), lambda g: (g, 0, 0)),
                  pl.BlockSpec((1, 2 * S5_STATE, wl), lambda g: (g, 0, 0)),
                  pl.BlockSpec((1, 2, 2 * S5_STATE), lambda g: (g, 0, 0)),
                  pl.BlockSpec((1, 1, wl), lambda g: (g, 0, 0))],
        out_specs=pl.BlockSpec((W, nc, L), lambda g: (g, 0, 0)),
        out_shape=jax.ShapeDtypeStruct((chans, nc, L), BF16),
        scratch_shapes=[pltpu.VMEM((wl, wl), BF16),
                        pltpu.VMEM((nc, 2 * S5_STATE), F32),
                        pltpu.VMEM((nc, 2 * S5_STATE), F32),
                        pltpu.VMEM((nc, 2 * S5_STATE), F32)],
        compiler_params=_cparams("parallel"),
        name="s5",
    )(u_t, ktab, m_in, m_out, a_pack, d_vec)


def _s5_tables(lam_re, lam_im, log_dt, b_re, b_im, c_re, c_im, d_skip):
    L = S5_CHUNK
    dt = jnp.exp(log_dt)[:, None]
    mag = jnp.exp(lam_re * dt)
    ang = lam_im * dt
    ar, ai = mag * jnp.cos(ang), mag * jnp.sin(ang)
    den = lam_re * lam_re + lam_im * lam_im
    cr = ((ar - 1.0) * lam_re + ai * lam_im) / den
    ci = (ai * lam_re - (ar - 1.0) * lam_im) / den
    bbr = cr[..., None] * b_re - ci[..., None] * b_im
    bbi = cr[..., None] * b_im + ci[..., None] * b_re
    pr, pi = jnp.ones_like(ar)[None], jnp.zeros_like(ar)[None]
    sr, si = ar, ai
    n = 1
    while n <= L:
        pr, pi = (jnp.concatenate([pr, pr * sr - pi * si], 0),
                  jnp.concatenate([pi, pr * si + pi * sr], 0))
        sr, si = sr * sr - si * si, 2.0 * sr * si
        n *= 2
    pr, pi = pr[:L + 1], pi[:L + 1]
    hp = lax.Precision.HIGHEST
    er = c_re[None] * pr[:L, :, None, :] - c_im[None] * pi[:L, :, None, :]
    ei = -(c_re[None] * pi[:L, :, None, :] + c_im[None] * pr[:L, :, None, :])
    ktab = (jnp.einsum('lgip,gpj->gjil', er, bbr, precision=hp)
            + jnp.einsum('lgip,gpj->gjil', ei, bbi, precision=hp))
    qr, qi = pr[:L][::-1], pi[:L][::-1]
    min_re = qr[:, :, :, None] * bbr[None] - qi[:, :, :, None] * bbi[None]
    min_im = qr[:, :, :, None] * bbi[None] + qi[:, :, :, None] * bbr[None]
    m_in = jnp.concatenate([min_re, min_im], axis=2)
    m_in = m_in.transpose(1, 3, 0, 2).reshape(m_in.shape[1], -1, 2 * S5_STATE)
    tr, ti = pr[1:L + 1], pi[1:L + 1]
    mo_re = c_re[None] * tr[:, :, None, :] - c_im[None] * ti[:, :, None, :]
    mo_im = -(c_re[None] * ti[:, :, None, :] + c_im[None] * tr[:, :, None, :])
    m_out = jnp.concatenate([mo_re, mo_im], axis=3)
    m_out = m_out.transpose(1, 3, 2, 0).reshape(m_out.shape[1], 2 * S5_STATE, -1)
    a_pack = jnp.stack([jnp.concatenate([pr[L], pr[L]], -1),
                        jnp.concatenate([-pi[L], pi[L]], -1)], axis=1)
    d_vec = jnp.repeat(d_skip, L, axis=-1)[:, None, :]
    groups, width = ktab.shape[:2]
    ktab = ktab.reshape(groups, width, width * L // LANES, LANES)
    return ktab, m_in.astype(BF16), m_out.astype(BF16), a_pack, d_vec


def _sgu_kernel(u_ref, v_ref, w_ref, bias_ref, lg_ref, lb_ref, o_ref, *, n_chunks):
    L = SGU_CHUNK
    row = lax.broadcasted_iota(jnp.int32, (L, L), 0)
    colm = lax.broadcasted_iota(jnp.int32, (L, L), 1)
    causal = colm <= row
    v = jax.nn.gelu(v_ref[0].astype(F32))
    mu = jnp.mean(v, axis=-1, keepdims=True)
    vc = v - mu
    var = jnp.mean(vc * vc, axis=-1, keepdims=True)
    vn = (vc * lax.rsqrt(var + EPS) * lg_ref[...] + lb_ref[...]).astype(BF16)
    gw = vn.shape[1] // SGU_GROUPS
    ws = [jnp.where(causal, w_ref[g], 0.0).astype(BF16) for g in range(SGU_GROUPS)]
    for c in range(n_chunks):
        rows = slice(c * L, (c + 1) * L)
        mixed = jnp.concatenate(
            [_dot(ws[g], vn[rows, g * gw:(g + 1) * gw]) for g in range(SGU_GROUPS)], axis=-1)
        u = jax.nn.gelu(u_ref[0, rows, :].astype(F32))
        o_ref[0, rows, :] = (u * (mixed + bias_ref[...])).astype(BF16)


def _sgu(proj, w_s, bias_full, ln_gain, ln_bias, u_col, v_col, tt):
    bsz, s, _ = proj.shape
    w = ln_gain.shape[1]
    return pl.pallas_call(
        functools.partial(_sgu_kernel, n_chunks=tt // SGU_CHUNK),
        grid=(bsz, s // tt),
        in_specs=[pl.BlockSpec((1, tt, w), lambda b, i: (b, i, u_col)),
                  pl.BlockSpec((1, tt, w), lambda b, i: (b, i, v_col)),
                  _const_spec(w_s.shape), _const_spec(bias_full.shape),
                  _const_spec((1, w)), _const_spec((1, w))],
        out_specs=pl.BlockSpec((1, tt, w), lambda b, i: (b, i, 0)),
        out_shape=jax.ShapeDtypeStruct((bsz, s, w), BF16),
        compiler_params=_cparams("parallel", "parallel"),
        name="sgu",
    )(proj, proj, w_s, bias_full, ln_gain, ln_bias)


def _tail_kernel(x_ref, a_ref, b_ref, wo_ref, g1_ref, sc_ref, sh_ref, g2_ref, w1_ref, w2_ref,
                 *rest, f_chunk, with_glu):
    if with_glu:
        wg_ref, bg_ref, o_ref = rest
        af = a_ref[...].astype(F32).T
        a = af.astype(BF16)
        a = (af * jax.nn.sigmoid(_dot(a, wg_ref[...]) + bg_ref[...])).astype(BF16)
    else:
        (o_ref,) = rest
        a = a_ref[0]
    half = a.shape[1]
    y = _dot(a, wo_ref[:half, :]) + _dot(b_ref[0], wo_ref[half:, :])
    x1 = x_ref[0] + g1_ref[0] * y
    h = _norm_mod(x1, sc_ref[0], sh_ref[0]).astype(BF16)
    f = w1_ref.shape[1]
    acc = None
    for f0 in range(0, f, f_chunk):
        hid = jnp.maximum(_dot(h, w1_ref[:, f0:f0 + f_chunk]), 0.0)
        part = _dot((hid * hid).astype(BF16), w2_ref[f0:f0 + f_chunk, :])
        acc = part if acc is None else acc + part
    o_ref[0] = x1 + g2_ref[0] * acc


def _tail(x, mix_a, mix_b, w_out, g1, sc, sh, g2, w1, w2, glu, tm):
    bsz, s, d = x.shape
    half = mix_b.shape[-1]
    f = w1.shape[1]
    nt = s // tm
    row = pl.BlockSpec((1, 1, d), lambda b, i: (b, 0, 0))
    if glu is not None:
        a_spec = pl.BlockSpec((half, tm), lambda b, i: (0, b * nt + i))
    else:
        a_spec = pl.BlockSpec((1, tm, half), lambda b, i: (b, i, 0))
    in_specs = [pl.BlockSpec((1, tm, d), lambda b, i: (b, i, 0)),
                a_spec,
                pl.BlockSpec((1, tm, half), lambda b, i: (b, i, 0)),
                _const_spec((d, d)), row, row, row, row,
                _const_spec((d, f)), _const_spec((f, d))]
    args = [x, mix_a, mix_b, w_out, g1, sc, sh, g2, w1, w2]
    if glu is not None:
        in_specs += [_const_spec((half, half)), _const_spec((1, half))]
        args += list(glu)
    return pl.pallas_call(
        functools.partial(_tail_kernel, f_chunk=1024, with_glu=glu is not None),
        grid=(bsz, nt),
        in_specs=in_specs,
        out_specs=pl.BlockSpec((1, tm, d), lambda b, i: (b, i, 0)),
        out_shape=jax.ShapeDtypeStruct((bsz, s, d), F32),
        compiler_params=_cparams("parallel", "parallel"),
        name="tail",
    )(*args)


def _tile(s, want):
    t = min(s, want)
    assert s % t == 0, (s, t)
    return t


def kernel(x, c, ada_w, ada_b, even_w_in, even_w_out, gla_w_lr, gla_b_lr, gla_gain, fox_b_f, fox_q_gain, fox_k_gain, odd_w_in, odd_w_out, s5_lam_re, s5_lam_im, s5_log_dt, s5_b_re, s5_b_im, s5_c_re, s5_c_im, s5_d, s5_w_glu, s5_b_glu, sgu_ln_gain, sgu_ln_bias, sgu_w_s, sgu_b_s, mlp_w1, mlp_w2):
    bsz, s, d = x.shape
    half = d // 2
    n_heads = half // HEAD_DIM
    tm = _tile(s, 512)

    mod = _ada_mod(c, ada_w, ada_b)

    def mods(layer):
        return [m[:, None, :] for m in jnp.split(mod[layer], 6, axis=-1)]

    sh1, sc1, g1, sh2, sc2, g2 = mods(0)
    w_in = even_w_in[0]
    o_lr = 4 * half
    o_fq = o_lr + GLA_RANK
    o_ff = o_fq + 3 * half
    w_main = jnp.concatenate([w_in[:, :o_lr], w_in[:, o_fq:o_ff]], axis=1).astype(BF16)
    w_small = jnp.concatenate(
        [w_in[:, o_lr:o_fq], w_in[:, o_ff:],
         jnp.zeros((d, LANES - GLA_RANK - n_heads), w_in.dtype)], axis=1).astype(BF16)
    proj, small = _inproj(x, sc1, sh1, w_main, (w_main.shape[1],), tm, w_small=w_small)

    wlr_pad = jnp.concatenate([gla_w_lr[0], jnp.zeros((LANES - GLA_RANK, half), F32)],
                              axis=0).astype(BF16)
    o_gla = _gla(proj, small, wlr_pad, gla_b_lr[0][None, :], gla_gain[0].reshape(1, half), tm)

    bf_pad = jnp.zeros((1, LANES), F32).at[0, FF_LANE0:FF_LANE0 + n_heads].set(fox_b_f[0])
    qp, kp, vt, cum = _fox_prep(proj, small, bf_pad, fox_q_gain[0].reshape(1, half),
                                fox_k_gain[0].reshape(1, half), 4, 5, 6, tm)
    cq, ce, thr = _fox_skip_tables(cum, fox_q_gain[0], fox_k_gain[0], n_heads)
    o_fox = _fox_attn(cq, ce, thr, qp, kp, vt, _tile(s, 1024))

    x = _tail(x, o_gla, o_fox, even_w_out[0].astype(BF16), g1, sc2, sh2, g2,
              mlp_w1[0].astype(BF16), mlp_w2[0].astype(BF16), None, tm)

    sh1, sc1, g1, sh2, sc2, g2 = mods(1)
    w_in = odd_w_in[0].astype(BF16)
    proj, u_t = _inproj(x, sc1, sh1, w_in[:, half:], (2 * half,), tm, w_t=w_in[:, :half].T)
    ktab, m_in, m_out, a_pack, d_vec = _s5_tables(
        s5_lam_re[0], s5_lam_im[0], s5_log_dt[0], s5_b_re[0], s5_b_im[0],
        s5_c_re[0], s5_c_im[0], s5_d[0])
    n_rows = bsz * s // S5_CHUNK
    y_t = _s5(u_t.reshape(half, n_rows, S5_CHUNK), ktab, m_in, m_out, a_pack, d_vec, bsz)
    y_t = y_t.reshape(half, bsz * s)

    bias_full = jnp.repeat(sgu_b_s[0].T, half // SGU_GROUPS, axis=1)
    y_sgu = _sgu(proj, sgu_w_s[0], bias_full, sgu_ln_gain[0][None, :], sgu_ln_bias[0][None, :],
                 0, 1, tm)

    x = _tail(x, y_t, y_sgu, odd_w_out[0].astype(BF16), g1, sc2, sh2, g2,
              mlp_w1[1].astype(BF16), mlp_w2[1].astype(BF16),
              (s5_w_glu[0].astype(BF16), s5_b_glu[0][None, :]), tm)
    return x
```

```python
import functools

import numpy as np
import jax
import jax.numpy as jnp
from jax import lax
from jax.experimental import pallas as pl
from jax.experimental.pallas import tpu as pltpu

F32 = jnp.float32
BF16 = jnp.bfloat16

EPS = 1e-6
HEAD_DIM = 64
LANES = 128
GLA_RANK = 16
GLA_TAU = 16.0
GLA_CHUNK = 64
S5_GROUP_WIDTH = 16
S5_STATE = 64
S5_CHUNK = 128
SGU_GROUPS = 8
SGU_CHUNK = 128
VMEM_LIMIT_BYTES = 56 * 1024 * 1024
NEG_BIG = -1e30
LOG2E = 1.4426950408889634


def _cparams(*sem):
    return pltpu.CompilerParams(dimension_semantics=sem, vmem_limit_bytes=VMEM_LIMIT_BYTES)


def _const_spec(shape):
    return pl.BlockSpec(shape, lambda *_: (0,) * len(shape), pipeline_mode=pl.Buffered(1))


def _dot(a, b):
    return jnp.dot(a, b, preferred_element_type=F32)


def _dot_nt(a, b):
    return lax.dot_general(a, b, (((1,), (1,)), ((), ())), preferred_element_type=F32)


def _dot_tn(a, b):
    return lax.dot_general(a, b, (((0,), (0,)), ((), ())), preferred_element_type=F32)


def _split2(x):
    hi = x.astype(BF16)
    lo = (x - hi.astype(F32)).astype(BF16)
    return hi, lo


def _split3(x):
    hi = x.astype(BF16)
    r = x - hi.astype(F32)
    mid = r.astype(BF16)
    lo = (r - mid.astype(F32)).astype(BF16)
    return hi, mid, lo


def _log_sigmoid(z):
    return jnp.minimum(z, 0.0) - jnp.log(1.0 + jnp.exp(-jnp.abs(z)))


def _head_mean_matrix():
    return jnp.asarray(np.kron(np.eye(LANES // HEAD_DIM, dtype=np.float32),
                               np.full((HEAD_DIM, HEAD_DIM), 1.0 / HEAD_DIM, np.float32)), BF16)


def _norm_mod(x, sc, sh):
    ms = jnp.mean(x * x, axis=-1, keepdims=True)
    return x * lax.rsqrt(ms + EPS) * (1.0 + sc) + sh


def _ada_kernel(c_ref, w_ref, b_ref, o_ref):
    c = c_ref[...]
    ca = c * jax.nn.sigmoid(c)
    o_ref[0] = _dot(ca, w_ref[0]) + b_ref[0]


def _ada_mod(c, ada_w, ada_b):
    depth, d, n = ada_w.shape
    bsz = c.shape[0]
    tn = n // 4
    return pl.pallas_call(
        _ada_kernel,
        grid=(depth, n // tn),
        in_specs=[pl.BlockSpec((bsz, d), lambda l, j: (0, 0)),
                  pl.BlockSpec((1, d, tn), lambda l, j: (l, 0, j)),
                  pl.BlockSpec((1, 1, tn), lambda l, j: (l, 0, j))],
        out_specs=pl.BlockSpec((1, bsz, tn), lambda l, j: (l, 0, j)),
        out_shape=jax.ShapeDtypeStruct((depth, bsz, n), F32),
        compiler_params=_cparams("parallel", "parallel"),
        name="ada_mod",
    )(c, ada_w, ada_b.reshape(depth, 1, n))


def _inproj_kernel(x_ref, sc_ref, sh_ref, *rest, widths, n_chunk, with_small):
    rest = list(rest)
    wm_ref = rest.pop(0)
    ws_ref = rest.pop(0) if with_small else None
    h = _norm_mod(x_ref[0], sc_ref[0], sh_ref[0]).astype(BF16)
    base = 0
    for width in widths:
        o_ref = rest.pop(0)
        for c0 in range(0, width, n_chunk):
            o_ref[0, :, c0:c0 + n_chunk] = _dot(
                h, wm_ref[:, base + c0:base + c0 + n_chunk]).astype(BF16)
        base += width
    if with_small:
        rest.pop(0)[0] = _dot(h, ws_ref[...])


def _inproj(x, sc, sh, w_main, widths, tm, w_small=None):
    bsz, s, d = x.shape
    n = w_main.shape[1]
    assert sum(widths) == n
    nt = s // tm
    in_specs = [pl.BlockSpec((1, tm, d), lambda b, i: (b, i, 0)),
                pl.BlockSpec((1, 1, d), lambda b, i: (b, 0, 0)),
                pl.BlockSpec((1, 1, d), lambda b, i: (b, 0, 0)),
                _const_spec((d, n))]
    args = [x, sc, sh, w_main]
    out_specs = [pl.BlockSpec((1, tm, wd), lambda b, i: (b, i, 0)) for wd in widths]
    out_shape = [jax.ShapeDtypeStruct((bsz, s, wd), BF16) for wd in widths]
    if w_small is not None:
        in_specs.append(_const_spec((d, LANES)))
        args.append(w_small)
        out_specs.append(pl.BlockSpec((1, tm, LANES), lambda b, i: (b, i, 0)))
        out_shape.append(jax.ShapeDtypeStruct((bsz, s, LANES), F32))
    return pl.pallas_call(
        functools.partial(_inproj_kernel, widths=tuple(widths), n_chunk=512,
                          with_small=w_small is not None),
        grid=(bsz, nt),
        in_specs=in_specs, out_specs=out_specs, out_shape=out_shape,
        compiler_params=_cparams("parallel", "parallel"),
        name="inproj",
    )(*args)


def _gla_kernel(q_ref, k_ref, v_ref, g_ref, sm_ref, wlr_ref, blr_ref, gain_ref, tril_ref, bd_ref,
                o_ref, state_ref, a_ref, kv_ref, st_ref, qin_ref, ebl_ref, o_acc_ref,
                *, n_heads, n_chunks):
    C = GLA_CHUNK
    mid = C // 2 - 1

    @pl.when(pl.program_id(1) == 0)
    def _():
        state_ref[...] = jnp.zeros_like(state_ref)

    row = lax.broadcasted_iota(jnp.int32, (C, C), 0)
    col = lax.broadcasted_iota(jnp.int32, (C, C), 1)
    causal = col <= row
    z = _dot(sm_ref[0].astype(BF16), wlr_ref[...]) + blr_ref[...]
    lc_hi, lc_lo = _split2(_log_sigmoid(z) * (1.0 / GLA_TAU))
    tril = tril_ref[...]
    bc_all = _dot(tril, lc_hi) + _dot(tril, lc_lo)
    heads = [slice(h * HEAD_DIM, (h + 1) * HEAD_DIM) for h in range(n_heads)]

    for c in range(n_chunks):
        rows = slice(c * C, (c + 1) * C)
        bc = bc_all[rows]
        m = bc[mid:mid + 1, :]
        bl = bc[C - 1:C, :]
        e = jnp.exp(bc - m)
        qt = q_ref[0, rows, :].astype(F32) * (HEAD_DIM ** -0.5) * e
        kt = k_ref[0, rows, :].astype(F32) * (1.0 / e)
        qin_ref[rows, :] = (qt * jnp.exp(m)).astype(BF16)
        kst = (kt * jnp.exp(bl - m)).astype(BF16)
        qt = qt.astype(BF16)
        kt = kt.astype(BF16)
        ebl_ref[c:c + 1, :] = jnp.exp(bl)
        v = v_ref[0, rows, :]
        for h, sl in enumerate(heads):
            a_ref[c, h] = jnp.where(causal, _dot_nt(qt[:, sl], kt[:, sl]), 0.0).astype(BF16)
            kv_ref[c, h] = _dot_tn(v[:, sl], kst[:, sl])

    for h, sl in enumerate(heads):
        st = state_ref[h]
        for c in range(n_chunks):
            st_ref[c, h] = st.astype(BF16)
            st = st * ebl_ref[c:c + 1, sl] + kv_ref[c, h]
        state_ref[h] = st

    for c in range(n_chunks):
        rows = slice(c * C, (c + 1) * C)
        for h, sl in enumerate(heads):
            o_acc_ref[rows, sl] = (_dot(a_ref[c, h], v_ref[0, rows, sl])
                                   + _dot_nt(qin_ref[rows, sl], st_ref[c, h]))

    bd = bd_ref[...]
    for p in range(n_heads * HEAD_DIM // LANES):
        cols = slice(p * LANES, (p + 1) * LANES)
        o = o_acc_ref[:, cols]
        ms = _dot((o * o).astype(BF16), bd)
        g = g_ref[0, :, cols].astype(F32)
        o_ref[0, :, cols] = (o * lax.rsqrt(ms + EPS) * gain_ref[:, cols]
                             * (g * jax.nn.sigmoid(g))).astype(BF16)


def _gla(proj, small, wlr_pad, blr, gain, tt):
    bsz, s, _ = proj.shape
    w = wlr_pad.shape[1]
    n_heads = w // HEAD_DIM

    def col(j):
        return pl.BlockSpec((1, tt, w), lambda b, i, j=j: (b, i, j))

    n_chunks = tt // GLA_CHUNK
    tril = jnp.asarray(np.kron(np.eye(n_chunks, dtype=np.float32),
                               np.tril(np.ones((GLA_CHUNK, GLA_CHUNK), np.float32))), BF16)
    bd = _head_mean_matrix()
    return pl.pallas_call(
        functools.partial(_gla_kernel, n_heads=n_heads, n_chunks=n_chunks),
        grid=(bsz, s // tt),
        in_specs=[col(0), col(1), col(2), col(3),
                  pl.BlockSpec((1, tt, LANES), lambda b, i: (b, i, 0)),
                  _const_spec((LANES, w)), _const_spec((1, w)), _const_spec((1, w)),
                  _const_spec((tt, tt)), _const_spec((LANES, LANES))],
        out_specs=pl.BlockSpec((1, tt, w), lambda b, i: (b, i, 0)),
        out_shape=jax.ShapeDtypeStruct((bsz, s, w), BF16),
        scratch_shapes=[pltpu.VMEM((n_heads, HEAD_DIM, HEAD_DIM), F32),
                        pltpu.VMEM((n_chunks, n_heads, GLA_CHUNK, GLA_CHUNK), BF16),
                        pltpu.VMEM((n_chunks, n_heads, HEAD_DIM, HEAD_DIM), F32),
                        pltpu.VMEM((n_chunks, n_heads, HEAD_DIM, HEAD_DIM), BF16),
                        pltpu.VMEM((tt, w), BF16),
                        pltpu.VMEM((n_chunks, w), F32),
                        pltpu.VMEM((tt, w), F32)],
        compiler_params=_cparams("parallel", "arbitrary"),
        name="gla",
    )(proj, proj, proj, proj, small, wlr_pad, blr, gain, tril, bd)


FF_LANE0 = GLA_RANK
C_LANE0 = HEAD_DIM
X_LANE0 = 96
SUB = 256
V_ROWS = HEAD_DIM + 16


def _fox_prep_kernel(q_ref, k_ref, v_ref, sm_ref, bf_ref, qg_ref, kg_ref, tril_ref, bd_ref,
                     sel_ref, cq_ref, ind_ref, qo_ref, ko_ref, vo_ref, co_ref, carry_ref, *, n_heads):
    @pl.when(pl.program_id(1) == 0)
    def _():
        carry_ref[...] = jnp.zeros_like(carry_ref)

    sm = sm_ref[0]
    tt = sm.shape[0]
    lane = lax.broadcasted_iota(jnp.int32, (tt, LANES), 1)
    is_ff = (lane >= FF_LANE0) & (lane < FF_LANE0 + n_heads)
    ls = jnp.where(is_ff, _log_sigmoid(sm + bf_ref[...]), 0.0)
    tril = tril_ref[...]
    p0, p1, p2 = _split3(ls)
    cum = carry_ref[0:1, :] + (_dot(tril, p0) + _dot(tril, p1) + _dot(tril, p2))
    carry_ref[0:1, :] = cum[tt - 1:tt, :]
    edge_rows = []
    for jb in range(tt // SUB):
        edge_rows += [cum[jb * SUB:jb * SUB + 1, :], cum[(jb + 1) * SUB - 1:(jb + 1) * SUB, :]]
    edge_rows.append(jnp.zeros((co_ref.shape[2] - len(edge_rows), LANES), F32))
    co_ref[0, 0] = jnp.concatenate(edge_rows, axis=0)
    c0, c1, c2 = _split3(cum * LOG2E)
    e = _dot(c0, sel_ref[0]) + _dot(c1, sel_ref[1]) + _dot(c2, sel_ref[2])
    ex_q = e[:, :LANES] + cq_ref[...]
    ek_all = e[:, LANES:]
    bd = bd_ref[...]
    low = lane < HEAD_DIM
    is_x = (lane >= X_LANE0) & (lane < X_LANE0 + 3)

    def normed(x_ref, gain_ref, scale, pair):
        x = x_ref[0, :, pair * LANES:(pair + 1) * LANES].astype(F32)
        ms = _dot((x * x).astype(BF16), bd)
        return x * lax.rsqrt(ms + EPS) * (gain_ref[:, pair * LANES:(pair + 1) * LANES] * scale)

    for pair in range(n_heads // 2):
        qn = normed(q_ref, qg_ref, HEAD_DIM ** -0.5 * LOG2E, pair)
        kn = normed(k_ref, kg_ref, 1.0, pair)
        for odd in range(2):
            h = 2 * pair + odd
            if odd:
                qh = pltpu.roll(qn, HEAD_DIM, axis=1)
                kh = pltpu.roll(kn, HEAD_DIM, axis=1)
            else:
                qh, kh = qn, kn
            ex_k = jnp.where(is_x, pltpu.roll(ek_all, (X_LANE0 - 16 * h) % LANES, axis=1),
                             ind_ref[h:h + 1, :])
            qo_ref[0, h] = jnp.where(low, qh, ex_q).astype(BF16)
            ko_ref[0, h] = jnp.where(low, kh, ex_k).astype(BF16)

    vt = v_ref[0].astype(F32).T
    ones_rows = (lax.broadcasted_iota(jnp.int32, (V_ROWS - HEAD_DIM, SUB), 0) == 0).astype(BF16)
    for h in range(n_heads):
        for jb in range(tt // SUB):
            vo_ref[0, h, jb, :HEAD_DIM, :] = vt[h * HEAD_DIM:(h + 1) * HEAD_DIM,
                                                jb * SUB:(jb + 1) * SUB].astype(BF16)
            vo_ref[0, h, jb, HEAD_DIM:, :] = ones_rows


def _fox_prep_consts(n_heads):
    sel = np.zeros((3, LANES, 2 * LANES), np.float32)
    cq = np.zeros((1, LANES), np.float32)
    ind = np.zeros((n_heads, LANES), np.float32)
    for h in range(n_heads):
        for r in range(3):
            sel[r, FF_LANE0 + h, C_LANE0 + 3 * h + r] = 1.0
            sel[r, FF_LANE0 + h, LANES + 16 * h + r] = -1.0
            ind[h, C_LANE0 + 3 * h + r] = 1.0
            cq[0, X_LANE0 + r] = 1.0
    return jnp.asarray(sel, BF16), jnp.asarray(cq), jnp.asarray(ind), _head_mean_matrix()


def _fox_prep(proj, small, bf_pad, q_gain, k_gain, q_col, k_col, v_col, tt):
    bsz, s, _ = proj.shape
    w = q_gain.shape[1]
    n_heads = w // HEAD_DIM
    sel, cq, ind, bd = _fox_prep_consts(n_heads)
    tril = jnp.asarray(np.tril(np.ones((tt, tt), np.float32)), BF16)
    out_spec = pl.BlockSpec((1, n_heads, tt, LANES), lambda b, i: (b, 0, i, 0))
    out_sds = jax.ShapeDtypeStruct((bsz, n_heads, s, LANES), BF16)

    def col(j):
        return pl.BlockSpec((1, tt, w), lambda b, i: (b, i, j))

    return pl.pallas_call(
        functools.partial(_fox_prep_kernel, n_heads=n_heads),
        grid=(bsz, s // tt),
        in_specs=[col(q_col), col(k_col), col(v_col),
                  pl.BlockSpec((1, tt, LANES), lambda b, i: (b, i, 0)),
                  _const_spec((1, LANES)), _const_spec((1, w)), _const_spec((1, w)),
                  _const_spec((tt, tt)), _const_spec((LANES, LANES)),
                  _const_spec(sel.shape), _const_spec(cq.shape), _const_spec(ind.shape)],
        out_specs=[out_spec, out_spec,
                   pl.BlockSpec((1, n_heads, tt // SUB, V_ROWS, SUB), lambda b, i: (b, 0, i, 0, 0)),
                   pl.BlockSpec((1, 1, 8, LANES), lambda b, i: (b, i, 0, 0))],
        out_shape=[out_sds, out_sds,
                   jax.ShapeDtypeStruct((bsz, n_heads, s // SUB, V_ROWS, SUB), BF16),
                   jax.ShapeDtypeStruct((bsz, s // tt, 8, LANES), F32)],
        scratch_shapes=[pltpu.VMEM((8, LANES), F32)],
        compiler_params=_cparams("parallel", "arbitrary"),
        name="fox_prep",
    )(proj, proj, proj, small, bf_pad, q_gain, k_gain, tril, bd, sel, cq, ind)


ZERO_MARGIN = 160.0
LOOKAHEAD = 2


def _fox_attn_kernel(cq_ref, ce_ref, thr_ref, q_ref, k_ref, vt_ref, o_ref, m_ref, acc_ref, s_ref,
                     tri_ref, *, n_sub):
    b = pl.program_id(0)
    pair = pl.program_id(1)
    i = pl.program_id(2)
    row = lax.broadcasted_iota(jnp.int32, (SUB, SUB), 0)
    col = lax.broadcasted_iota(jnp.int32, (SUB, SUB), 1)
    tri_ref[...] = jnp.where(row <= col, 0.0, NEG_BIG)

    def blocks(r):
        a = i * n_sub + r
        return a, jnp.maximum(a - 1, 0)

    chains = [(r, hh) for r in range(n_sub) for hh in range(2)]

    def scores(r, hh):
        a, jp = blocks(r)
        q = q_ref[0, hh, r * SUB:(r + 1) * SUB, :]
        s_ref[r, hh, :SUB, :] = _dot_nt(k_ref[0, hh, pl.ds(pl.multiple_of(jp * SUB, SUB), SUB), :], q)
        s_ref[r, hh, SUB:, :] = _dot_nt(k_ref[0, hh, pl.ds(pl.multiple_of(a * SUB, SUB), SUB), :], q)

    def softmax_pv(r, hh):
        a, jp = blocks(r)
        no_prev = jnp.where(a == 0, NEG_BIG, 0.0)
        st = s_ref[r, hh, :SUB, :] + no_prev
        sd = s_ref[r, hh, SUB:, :] + tri_ref[...]
        m = jnp.maximum(jnp.max(st, axis=0, keepdims=True), jnp.max(sd, axis=0, keepdims=True))
        pt = jnp.exp2((st - m).astype(BF16))
        pd = jnp.exp2((sd - m).astype(BF16))
        acc_ref[r, hh] = _dot(vt_ref[0, hh, jp], pt) + _dot(vt_ref[0, hh, a], pd)
        m_ref[r, hh, 0:1, :] = m

    for n in range(len(chains) + LOOKAHEAD):
        if n < len(chains):
            scores(*chains[n])
        if n >= LOOKAHEAD:
            softmax_pv(*chains[n - LOOKAHEAD])

    def reaches(r, hh, j):
        h = 2 * pair + hh
        a = i * n_sub + r
        return (j >= 0) & (cq_ref[b, h, a] - ce_ref[b, h, jnp.maximum(j, 0)] >= thr_ref[h])

    any_more = functools.reduce(jnp.logical_or,
                                [reaches(r, hh, i * n_sub + r - 2) for r, hh in chains])

    @pl.when(any_more)
    def _():
        for r, hh in chains:
            q = q_ref[0, hh, r * SUB:(r + 1) * SUB, :]

            def body(c, hh=hh, q=q):
                j, m, acc = c
                kb = k_ref[0, hh, pl.ds(pl.multiple_of(j * SUB, SUB), SUB), :]
                s = _dot_nt(kb, q)
                m_new = jnp.maximum(m, jnp.max(s, axis=0, keepdims=True))
                p = jnp.exp2((s - m_new).astype(BF16))
                return j - 1, m_new, jnp.exp2(m - m_new) * acc + _dot(vt_ref[0, hh, j], p)

            _, m, acc = lax.while_loop(lambda c, r=r, hh=hh: reaches(r, hh, c[0]), body,
                                       (i * n_sub + r - 2, m_ref[r, hh, 0:1, :], acc_ref[r, hh]))
            acc_ref[r, hh] = acc
            m_ref[r, hh, 0:1, :] = m

    for r in range(n_sub):
        outs = [acc_ref[r, hh, :HEAD_DIM, :] / acc_ref[r, hh, HEAD_DIM:HEAD_DIM + 1, :]
                for hh in range(2)]
        o_ref[0, r * SUB:(r + 1) * SUB, :] = jnp.concatenate(outs, axis=0).T.astype(BF16)


def _fox_attn(cq, ce, thr, qp, kp, vt, tq):
    bsz, n_heads, s, _ = qp.shape
    nkb = s // SUB
    n_sub = tq // SUB
    grid_spec = pltpu.PrefetchScalarGridSpec(
        num_scalar_prefetch=3,
        grid=(bsz, n_heads // 2, s // tq),
        in_specs=[pl.BlockSpec((1, 2, tq, LANES), lambda b, p, i, *_: (b, p, i, 0)),
                  pl.BlockSpec((1, 2, s, LANES), lambda b, p, i, *_: (b, p, 0, 0)),
                  pl.BlockSpec((1, 2, nkb, V_ROWS, SUB), lambda b, p, i, *_: (b, p, 0, 0, 0))],
        out_specs=pl.BlockSpec((1, tq, LANES), lambda b, p, i, *_: (b, i, p)),
        scratch_shapes=[pltpu.VMEM((n_sub, 2, 8, SUB), F32),
                        pltpu.VMEM((n_sub, 2, V_ROWS, SUB), F32),
                        pltpu.VMEM((n_sub, 2, 2 * SUB, SUB), F32),
                        pltpu.VMEM((SUB, SUB), F32)])
    return pl.pallas_call(
        functools.partial(_fox_attn_kernel, n_sub=n_sub),
        grid_spec=grid_spec,
        out_shape=jax.ShapeDtypeStruct((bsz, s, n_heads * HEAD_DIM), BF16),
        compiler_params=_cparams("parallel", "parallel", "arbitrary"),
        name="fox_attn",
    )(cq, ce, thr, qp, kp, vt)


def _fox_skip_tables(edges, q_gain, k_gain, n_heads, blocks):
    bsz, nt = edges.shape[:2]
    c2 = edges[:, :, :2 * blocks, FF_LANE0:FF_LANE0 + n_heads] * LOG2E
    c2 = c2.reshape(bsz, nt * blocks, 2, n_heads)
    cq = c2[:, :, 0, :].transpose(0, 2, 1)
    ce = c2[:, :, 1, :].transpose(0, 2, 1)
    bound = (jnp.max(jnp.abs(q_gain), axis=-1) * jnp.max(jnp.abs(k_gain), axis=-1)
             * (HEAD_DIM ** 0.5) * LOG2E)
    thr = -(2.0 * 1.02 * bound + ZERO_MARGIN)
    return cq, ce, thr


def _s5_kernel(u_ref, kt_ref, min_ref, mout_ref, a_ref, d_ref, o_ref,
               t_ref, v_ref, vs_ref, xp_ref, *, bsz, n_col):
    L = S5_CHUNK
    W = S5_GROUP_WIDTH
    per = LANES // L
    nc = u_ref.shape[1]
    row = lax.broadcasted_iota(jnp.int32, (L, LANES), 0)
    lane = lax.broadcasted_iota(jnp.int32, (L, LANES), 1)
    causal = (lane % L) >= row

    tiles_per_chunk = n_col // LANES

    def build(chunk):
        for j in range(W):
            for ip in range(chunk * tiles_per_chunk, (chunk + 1) * tiles_per_chunk):
                tile = jnp.broadcast_to(kt_ref[0, j, ip:ip + 1, :], (L, LANES))
                tile = pltpu.roll(tile, 0, axis=1, stride=1, stride_axis=0)
                t_ref[j * L:(j + 1) * L, ip * LANES:(ip + 1) * LANES] = (
                    jnp.where(causal, tile, 0.0).astype(BF16))

    build(0)

    u = jnp.concatenate([u_ref[j] for j in range(W)], axis=-1)
    v = _dot(u, min_ref[0])
    v_ref[...] = v
    vs_ref[...] = pltpu.roll(v, S5_STATE, axis=1)
    a1 = a_ref[0, 0:1, :]
    a2 = a_ref[0, 1:2, :]

    n_steps = nc // bsz

    def scan(c, carry):
        x, xs = carry
        rows = pl.ds(c, bsz, stride=n_steps)
        xp_ref[rows, :] = x
        return (x * a1 + xs * a2 + v_ref[rows, :], xs * a1 - x * a2 + vs_ref[rows, :])

    zero = jnp.zeros((bsz, 2 * S5_STATE), F32)
    lax.fori_loop(0, n_steps, scan, (zero, zero), unroll=8)

    xp = xp_ref[...].astype(BF16)
    for c0 in range(0, W * L, n_col):
        cs = slice(c0, c0 + n_col)
        if c0 + n_col < W * L:
            build(c0 // n_col + 1)
        y = (_dot(u, t_ref[:, cs]) + _dot(xp, mout_ref[0, :, cs])
             + d_ref[0, :, cs] * u[:, cs].astype(F32))
        y = jax.nn.gelu(y).astype(BF16)
        for k in range(n_col // L):
            o_ref[c0 // L + k] = y[:, k * L:(k + 1) * L]


def _s5(u_t, ktab, m_in, m_out, a_pack, d_vec, bsz):
    chans, nc, L = u_t.shape
    W = S5_GROUP_WIDTH
    wl = W * L
    return pl.pallas_call(
        functools.partial(_s5_kernel, bsz=bsz, n_col=512),
        grid=(chans // W,),
        in_specs=[pl.BlockSpec((W, nc, L), lambda g: (g, 0, 0)),
                  pl.BlockSpec((1,) + ktab.shape[1:], lambda g: (g, 0, 0, 0)),
                  pl.BlockSpec((1, wl, 2 * S5_STATE), lambda g: (g, 0, 0)),
                  pl.BlockSpec((1, 2 * S5_STATE, wl), lambda g: (g, 0, 0)),
                  pl.BlockSpec((1, 2, 2 * S5_STATE), lambda g: (g, 0, 0)),
                  pl.BlockSpec((1, 1, wl), lambda g: (g, 0, 0))],
        out_specs=pl.BlockSpec((W, nc, L), lambda g: (g, 0, 0)),
        out_shape=jax.ShapeDtypeStruct((chans, nc, L), BF16),
        scratch_shapes=[pltpu.VMEM((wl, wl), BF16),
                        pltpu.VMEM((nc, 2 * S5_STATE), F32),
                        pltpu.VMEM((nc, 2 * S5_STATE), F32),
                        pltpu.VMEM((nc, 2 * S5_STATE), F32)],
        compiler_params=_cparams("parallel"),
        name="s5",
    )(u_t, ktab, m_in, m_out, a_pack, d_vec)


def _s5_tables(lam_re, lam_im, log_dt, b_re, b_im, c_re, c_im, d_skip):
    L = S5_CHUNK
    dt = jnp.exp(log_dt)[:, None]
    mag = jnp.exp(lam_re * dt)
    ang = lam_im * dt
    ar, ai = mag * jnp.cos(ang), mag * jnp.sin(ang)
    den = lam_re * lam_re + lam_im * lam_im
    cr = ((ar - 1.0) * lam_re + ai * lam_im) / den
    ci = (ai * lam_re - (ar - 1.0) * lam_im) / den
    bbr = cr[..., None] * b_re - ci[..., None] * b_im
    bbi = cr[..., None] * b_im + ci[..., None] * b_re
    pr, pi = jnp.ones_like(ar)[None], jnp.zeros_like(ar)[None]
    sr, si = ar, ai
    n = 1
    while n <= L:
        pr, pi = (jnp.concatenate([pr, pr * sr - pi * si], 0),
                  jnp.concatenate([pi, pr * si + pi * sr], 0))
        sr, si = sr * sr - si * si, 2.0 * sr * si
        n *= 2
    pr, pi = pr[:L + 1], pi[:L + 1]
    hp = lax.Precision.HIGHEST
    er = c_re[None] * pr[:L, :, None, :] - c_im[None] * pi[:L, :, None, :]
    ei = -(c_re[None] * pi[:L, :, None, :] + c_im[None] * pr[:L, :, None, :])
    ktab = (jnp.einsum('lgip,gpj->gjil', er, bbr, precision=hp)
            + jnp.einsum('lgip,gpj->gjil', ei, bbi, precision=hp))
    qr, qi = pr[:L][::-1], pi[:L][::-1]
    min_re = qr[:, :, :, None] * bbr[None] - qi[:, :, :, None] * bbi[None]
    min_im = qr[:, :, :, None] * bbi[None] + qi[:, :, :, None] * bbr[None]
    m_in = jnp.concatenate([min_re, min_im], axis=2)
    m_in = m_in.transpose(1, 3, 0, 2).reshape(m_in.shape[1], -1, 2 * S5_STATE)
    tr, ti = pr[1:L + 1], pi[1:L + 1]
    mo_re = c_re[None] * tr[:, :, None, :] - c_im[None] * ti[:, :, None, :]
    mo_im = -(c_re[None] * ti[:, :, None, :] + c_im[None] * tr[:, :, None, :])
    m_out = jnp.concatenate([mo_re, mo_im], axis=3)
    m_out = m_out.transpose(1, 3, 2, 0).reshape(m_out.shape[1], 2 * S5_STATE, -1)
    a_pack = jnp.stack([jnp.concatenate([pr[L], pr[L]], -1),
                        jnp.concatenate([-pi[L], pi[L]], -1)], axis=1)
    d_vec = jnp.repeat(d_skip, L, axis=-1)[:, None, :]
    groups, width = ktab.shape[:2]
    ktab = ktab.reshape(groups, width, width * L // LANES, LANES)
    return ktab, m_in.astype(BF16), m_out.astype(BF16), a_pack, d_vec


def _sgu_math(zu, zv, w_ref, bias_ref, lg_ref, lb_ref, o_ref):
    L = SGU_CHUNK
    row = lax.broadcasted_iota(jnp.int32, (L, L), 0)
    colm = lax.broadcasted_iota(jnp.int32, (L, L), 1)
    causal = colm <= row
    v = jax.nn.gelu(zv)
    mu = jnp.mean(v, axis=-1, keepdims=True)
    vc = v - mu
    var = jnp.mean(vc * vc, axis=-1, keepdims=True)
    vn = (vc * lax.rsqrt(var + EPS) * lg_ref[...] + lb_ref[...]).astype(BF16)
    gw = vn.shape[1] // SGU_GROUPS
    ws = [jnp.where(causal, w_ref[g], 0.0).astype(BF16) for g in range(SGU_GROUPS)]
    for c in range(vn.shape[0] // L):
        rows = slice(c * L, (c + 1) * L)
        mixed = jnp.concatenate(
            [_dot(ws[g], vn[rows, g * gw:(g + 1) * gw]) for g in range(SGU_GROUPS)], axis=-1)
        o_ref[0, rows, :] = (jax.nn.gelu(zu[rows]) * (mixed + bias_ref[...])).astype(BF16)


def _inproj_sgu_kernel(x_ref, sc_ref, sh_ref, wm_ref, wt_ref, w_ref, bias_ref, lg_ref, lb_ref,
                       o_ref, ot_ref):
    h = _norm_mod(x_ref[0], sc_ref[0], sh_ref[0]).astype(BF16)
    half = o_ref.shape[2]
    zv = _dot(h, wm_ref[:, half:])
    zu = _dot(h, wm_ref[:, :half])
    _sgu_math(zu, zv, w_ref, bias_ref, lg_ref, lb_ref, o_ref)
    ot_ref[...] = _dot_nt(wt_ref[...], h).astype(BF16)


def _inproj_sgu(x, sc, sh, w_sgu, w_t, w_s, bias_full, ln_gain, ln_bias, tm):
    bsz, s, d = x.shape
    half = w_t.shape[0]
    nt = s // tm
    return pl.pallas_call(
        _inproj_sgu_kernel,
        grid=(bsz, nt),
        in_specs=[pl.BlockSpec((1, tm, d), lambda b, i: (b, i, 0)),
                  pl.BlockSpec((1, 1, d), lambda b, i: (b, 0, 0)),
                  pl.BlockSpec((1, 1, d), lambda b, i: (b, 0, 0)),
                  _const_spec(w_sgu.shape), _const_spec(w_t.shape),
                  _const_spec(w_s.shape), _const_spec(bias_full.shape),
                  _const_spec((1, half)), _const_spec((1, half))],
        out_specs=[pl.BlockSpec((1, tm, half), lambda b, i: (b, i, 0)),
                   pl.BlockSpec((half, tm), lambda b, i: (0, b * nt + i))],
        out_shape=[jax.ShapeDtypeStruct((bsz, s, half), BF16),
                   jax.ShapeDtypeStruct((half, bsz * s), BF16)],
        compiler_params=_cparams("parallel", "parallel"),
        name="inproj_sgu",
    )(x, sc, sh, w_sgu, w_t, w_s, bias_full, ln_gain, ln_bias)


def _tail_kernel(x_ref, a_ref, b_ref, wo_ref, g1_ref, sc_ref, sh_ref, g2_ref, w1_ref, w2_ref,
                 *rest, f_chunk, with_glu):
    if with_glu:
        wg_ref, bg_ref, o_ref = rest
        af = a_ref[...].astype(F32).T
        a = af.astype(BF16)
        a = (af * jax.nn.sigmoid(_dot(a, wg_ref[...]) + bg_ref[...])).astype(BF16)
    else:
        (o_ref,) = rest
        a = a_ref[0]
    half = a.shape[1]
    y = _dot(a, wo_ref[:half, :]) + _dot(b_ref[0], wo_ref[half:, :])
    x1 = x_ref[0] + g1_ref[0] * y
    h = _norm_mod(x1, sc_ref[0], sh_ref[0]).astype(BF16)
    f = w1_ref.shape[1]
    acc = None
    for f0 in range(0, f, f_chunk):
        hid = jnp.maximum(_dot(h, w1_ref[:, f0:f0 + f_chunk]), 0.0)
        part = _dot((hid * hid).astype(BF16), w2_ref[f0:f0 + f_chunk, :])
        acc = part if acc is None else acc + part
    o_ref[0] = x1 + g2_ref[0] * acc


def _tail(x, mix_a, mix_b, w_out, g1, sc, sh, g2, w1, w2, glu, tm):
    bsz, s, d = x.shape
    half = mix_b.shape[-1]
    f = w1.shape[1]
    nt = s // tm
    row = pl.BlockSpec((1, 1, d), lambda b, i: (b, 0, 0))
    if glu is not None:
        a_spec = pl.BlockSpec((half, tm), lambda b, i: (0, b * nt + i))
    else:
        a_spec = pl.BlockSpec((1, tm, half), lambda b, i: (b, i, 0))
    in_specs = [pl.BlockSpec((1, tm, d), lambda b, i: (b, i, 0)),
                a_spec,
                pl.BlockSpec((1, tm, half), lambda b, i: (b, i, 0)),
                _const_spec((d, d)), row, row, row, row,
                _const_spec((d, f)), _const_spec((f, d))]
    args = [x, mix_a, mix_b, w_out, g1, sc, sh, g2, w1, w2]
    if glu is not None:
        in_specs += [_const_spec((half, half)), _const_spec((1, half))]
        args += list(glu)
    return pl.pallas_call(
        functools.partial(_tail_kernel, f_chunk=1024, with_glu=glu is not None),
        grid=(bsz, nt),
        in_specs=in_specs,
        out_specs=pl.BlockSpec((1, tm, d), lambda b, i: (b, i, 0)),
        out_shape=jax.ShapeDtypeStruct((bsz, s, d), F32),
        compiler_params=_cparams("parallel", "parallel"),
        name="tail",
    )(*args)


def _tile(s, want):
    t = min(s, want)
    assert s % t == 0, (s, t)
    return t


def kernel(x, c, ada_w, ada_b, even_w_in, even_w_out, gla_w_lr, gla_b_lr, gla_gain, fox_b_f, fox_q_gain, fox_k_gain, odd_w_in, odd_w_out, s5_lam_re, s5_lam_im, s5_log_dt, s5_b_re, s5_b_im, s5_c_re, s5_c_im, s5_d, s5_w_glu, s5_b_glu, sgu_ln_gain, sgu_ln_bias, sgu_w_s, sgu_b_s, mlp_w1, mlp_w2):
    bsz, s, d = x.shape
    half = d // 2
    n_heads = half // HEAD_DIM
    tm = _tile(s, 512)

    mod = _ada_mod(c, ada_w, ada_b)

    def mods(layer):
        return [m[:, None, :] for m in jnp.split(mod[layer], 6, axis=-1)]

    sh1, sc1, g1, sh2, sc2, g2 = mods(0)
    w_in = even_w_in[0]
    o_lr = 4 * half
    o_fq = o_lr + GLA_RANK
    o_ff = o_fq + 3 * half
    w_main = jnp.concatenate([w_in[:, :o_lr], w_in[:, o_fq:o_ff]], axis=1).astype(BF16)
    w_small = jnp.concatenate(
        [w_in[:, o_lr:o_fq], w_in[:, o_ff:],
         jnp.zeros((d, LANES - GLA_RANK - n_heads), w_in.dtype)], axis=1).astype(BF16)
    proj, small = _inproj(x, sc1, sh1, w_main, (w_main.shape[1],), tm, w_small=w_small)

    wlr_pad = jnp.concatenate([gla_w_lr[0], jnp.zeros((LANES - GLA_RANK, half), F32)],
                              axis=0).astype(BF16)
    o_gla = _gla(proj, small, wlr_pad, gla_b_lr[0][None, :], gla_gain[0].reshape(1, half), tm)

    bf_pad = jnp.zeros((1, LANES), F32).at[0, FF_LANE0:FF_LANE0 + n_heads].set(fox_b_f[0])
    qp, kp, vt, edges = _fox_prep(proj, small, bf_pad, fox_q_gain[0].reshape(1, half),
                                  fox_k_gain[0].reshape(1, half), 4, 5, 6, tm)
    cq, ce, thr = _fox_skip_tables(edges, fox_q_gain[0], fox_k_gain[0], n_heads, tm // SUB)
    o_fox = _fox_attn(cq, ce, thr, qp, kp, vt, _tile(s, 2048))

    x = _tail(x, o_gla, o_fox, even_w_out[0].astype(BF16), g1, sc2, sh2, g2,
              mlp_w1[0].astype(BF16), mlp_w2[0].astype(BF16), None, tm)

    sh1, sc1, g1, sh2, sc2, g2 = mods(1)
    w_in = odd_w_in[0].astype(BF16)
    bias_full = jnp.repeat(sgu_b_s[0].T, half // SGU_GROUPS, axis=1)
    y_sgu, u_t = _inproj_sgu(x, sc1, sh1, w_in[:, half:], w_in[:, :half].T, sgu_w_s[0], bias_full,
                             sgu_ln_gain[0][None, :], sgu_ln_bias[0][None, :], tm)
    ktab, m_in, m_out, a_pack, d_vec = _s5_tables(
        s5_lam_re[0], s5_lam_im[0], s5_log_dt[0], s5_b_re[0], s5_b_im[0],
        s5_c_re[0], s5_c_im[0], s5_d[0])
    n_rows = bsz * s // S5_CHUNK
    y_t = _s5(u_t.reshape(half, n_rows, S5_CHUNK), ktab, m_in, m_out, a_pack, d_vec, bsz)
    y_t = y_t.reshape(half, bsz * s)

    x = _tail(x, y_t, y_sgu, odd_w_out[0].astype(BF16), g1, sc2, sh2, g2,
              mlp_w1[1].astype(BF16), mlp_w2[1].astype(BF16),
              (s5_w_glu[0].astype(BF16), s5_b_glu[0][None, :]), tm)
    return x
```

```python
import functools

import numpy as np
import jax
import jax.numpy as jnp
from jax import lax
from jax.experimental import pallas as pl
from jax.experimental.pallas import tpu as pltpu

F32 = jnp.float32
BF16 = jnp.bfloat16

EPS = 1e-6
HEAD_DIM = 64
LANES = 128
GLA_RANK = 16
GLA_TAU = 16.0
GLA_CHUNK = 64
S5_GROUP_WIDTH = 16
S5_STATE = 64
S5_CHUNK = 128
SGU_GROUPS = 8
SGU_CHUNK = 128
VMEM_LIMIT_BYTES = 56 * 1024 * 1024
NEG_BIG = -1e30
LOG2E = 1.4426950408889634


def _cparams(*sem):
    return pltpu.CompilerParams(dimension_semantics=sem, vmem_limit_bytes=VMEM_LIMIT_BYTES)


def _const_spec(shape):
    return pl.BlockSpec(shape, lambda *_: (0,) * len(shape), pipeline_mode=pl.Buffered(1))


def _dot(a, b):
    return jnp.dot(a, b, preferred_element_type=F32)


def _dot_nt(a, b):
    return lax.dot_general(a, b, (((1,), (1,)), ((), ())), preferred_element_type=F32)


def _dot_tn(a, b):
    return lax.dot_general(a, b, (((0,), (0,)), ((), ())), preferred_element_type=F32)


def _split2(x):
    hi = x.astype(BF16)
    lo = (x - hi.astype(F32)).astype(BF16)
    return hi, lo


def _split3(x):
    hi = x.astype(BF16)
    r = x - hi.astype(F32)
    mid = r.astype(BF16)
    lo = (r - mid.astype(F32)).astype(BF16)
    return hi, mid, lo


def _log_sigmoid(z):
    return jnp.minimum(z, 0.0) - jnp.log(1.0 + jnp.exp(-jnp.abs(z)))


def _head_mean_matrix():
    return jnp.asarray(np.kron(np.eye(LANES // HEAD_DIM, dtype=np.float32),
                               np.full((HEAD_DIM, HEAD_DIM), 1.0 / HEAD_DIM, np.float32)), BF16)


def _norm_mod(x, sc, sh):
    ms = jnp.mean(x * x, axis=-1, keepdims=True)
    return x * lax.rsqrt(ms + EPS) * (1.0 + sc) + sh


def _ada_kernel(c_ref, w_ref, b_ref, o_ref):
    c = c_ref[...]
    ca = c * jax.nn.sigmoid(c)
    o_ref[0] = _dot(ca, w_ref[0]) + b_ref[0]


def _ada_mod(c, ada_w, ada_b):
    depth, d, n = ada_w.shape
    bsz = c.shape[0]
    tn = n // 4
    return pl.pallas_call(
        _ada_kernel,
        grid=(depth, n // tn),
        in_specs=[pl.BlockSpec((bsz, d), lambda l, j: (0, 0)),
                  pl.BlockSpec((1, d, tn), lambda l, j: (l, 0, j)),
                  pl.BlockSpec((1, 1, tn), lambda l, j: (l, 0, j))],
        out_specs=pl.BlockSpec((1, bsz, tn), lambda l, j: (l, 0, j)),
        out_shape=jax.ShapeDtypeStruct((depth, bsz, n), F32),
        compiler_params=_cparams("parallel", "parallel"),
        name="ada_mod",
    )(c, ada_w, ada_b.reshape(depth, 1, n))


def _gla_kernel(q_ref, k_ref, v_ref, g_ref, sm_ref, wlr_ref, blr_ref, gain_ref, tril_ref, bd_ref,
                o_ref, state_ref, a_ref, kv_ref, st_ref, qin_ref, ebl_ref, o_acc_ref,
                *, n_heads, n_chunks):
    C = GLA_CHUNK
    mid = C // 2 - 1

    @pl.when(pl.program_id(1) == 0)
    def _():
        state_ref[...] = jnp.zeros_like(state_ref)

    row = lax.broadcasted_iota(jnp.int32, (C, C), 0)
    col = lax.broadcasted_iota(jnp.int32, (C, C), 1)
    causal = col <= row
    z = _dot(sm_ref[0].astype(BF16), wlr_ref[...]) + blr_ref[...]
    lc = (_log_sigmoid(z) * (1.0 / GLA_TAU)).astype(BF16)
    bc_all = _dot(tril_ref[...], lc)
    heads = [slice(h * HEAD_DIM, (h + 1) * HEAD_DIM) for h in range(n_heads)]

    for c in range(n_chunks):
        rows = slice(c * C, (c + 1) * C)
        bc = bc_all[rows]
        m = bc[mid:mid + 1, :]
        bl = bc[C - 1:C, :]
        e = jnp.exp(bc - m)
        qt = q_ref[0, rows, :].astype(F32) * (HEAD_DIM ** -0.5) * e
        kt = k_ref[0, rows, :].astype(F32) * (1.0 / e)
        qin_ref[rows, :] = (qt * jnp.exp(m)).astype(BF16)
        kst = (kt * jnp.exp(bl - m)).astype(BF16)
        qt = qt.astype(BF16)
        kt = kt.astype(BF16)
        ebl_ref[c:c + 1, :] = jnp.exp(bl)
        v = v_ref[0, rows, :]
        for h, sl in enumerate(heads):
            a_ref[c, h] = jnp.where(causal, _dot_nt(qt[:, sl], kt[:, sl]), 0.0).astype(BF16)
            kv_ref[c, h] = _dot_tn(v[:, sl], kst[:, sl])

    for h, sl in enumerate(heads):
        st = state_ref[h]
        for c in range(n_chunks):
            st_ref[c, h] = st.astype(BF16)
            st = st * ebl_ref[c:c + 1, sl] + kv_ref[c, h]
        state_ref[h] = st

    for c in range(n_chunks):
        rows = slice(c * C, (c + 1) * C)
        for h, sl in enumerate(heads):
            o_acc_ref[rows, sl] = (_dot(a_ref[c, h], v_ref[0, rows, sl])
                                   + _dot_nt(qin_ref[rows, sl], st_ref[c, h]))

    bd = bd_ref[...]
    for p in range(n_heads * HEAD_DIM // LANES):
        cols = slice(p * LANES, (p + 1) * LANES)
        o = o_acc_ref[:, cols]
        ms = _dot((o * o).astype(BF16), bd)
        g = g_ref[0, :, cols].astype(F32)
        o_ref[0, :, cols] = (o * lax.rsqrt(ms + EPS) * gain_ref[:, cols]
                             * (g * jax.nn.sigmoid(g))).astype(BF16)


def _gla(proj, small, wlr_pad, blr, gain, tt):
    bsz, s, _ = proj.shape
    w = wlr_pad.shape[1]
    n_heads = w // HEAD_DIM

    def col(j):
        return pl.BlockSpec((1, tt, w), lambda b, i, j=j: (b, i, j))

    n_chunks = tt // GLA_CHUNK
    tril = jnp.asarray(np.kron(np.eye(n_chunks, dtype=np.float32),
                               np.tril(np.ones((GLA_CHUNK, GLA_CHUNK), np.float32))), BF16)
    bd = _head_mean_matrix()
    return pl.pallas_call(
        functools.partial(_gla_kernel, n_heads=n_heads, n_chunks=n_chunks),
        grid=(bsz, s // tt),
        in_specs=[col(0), col(1), col(2), col(3),
                  pl.BlockSpec((1, tt, LANES), lambda b, i: (b, i, 0)),
                  _const_spec((LANES, w)), _const_spec((1, w)), _const_spec((1, w)),
                  _const_spec((tt, tt)), _const_spec((LANES, LANES))],
        out_specs=pl.BlockSpec((1, tt, w), lambda b, i: (b, i, 0)),
        out_shape=jax.ShapeDtypeStruct((bsz, s, w), BF16),
        scratch_shapes=[pltpu.VMEM((n_heads, HEAD_DIM, HEAD_DIM), F32),
                        pltpu.VMEM((n_chunks, n_heads, GLA_CHUNK, GLA_CHUNK), BF16),
                        pltpu.VMEM((n_chunks, n_heads, HEAD_DIM, HEAD_DIM), F32),
                        pltpu.VMEM((n_chunks, n_heads, HEAD_DIM, HEAD_DIM), BF16),
                        pltpu.VMEM((tt, w), BF16),
                        pltpu.VMEM((n_chunks, w), F32),
                        pltpu.VMEM((tt, w), F32)],
        compiler_params=_cparams("parallel", "arbitrary"),
        name="gla",
    )(proj, proj, proj, proj, small, wlr_pad, blr, gain, tril, bd)


FF_LANE0 = GLA_RANK
C_LANE0 = HEAD_DIM
X_LANE0 = 96
SUB = 256
V_ROWS = HEAD_DIM + 16


def _fox_prep_math(fq, fk, vt, sm, bf_ref, qg_ref, kg_ref, tril_ref, bd_ref, sel_ref, cq_ref, ind_ref,
                   qo_ref, ko_ref, vo_ref, co_ref, carry_ref, n_heads):
    tt = sm.shape[0]
    lane = lax.broadcasted_iota(jnp.int32, (tt, LANES), 1)
    is_ff = (lane >= FF_LANE0) & (lane < FF_LANE0 + n_heads)
    ls = jnp.where(is_ff, _log_sigmoid(sm + bf_ref[...]), 0.0)
    tril = tril_ref[...]
    p0, p1, p2 = _split3(ls)
    cum = carry_ref[0:1, :] + (_dot(tril, p0) + _dot(tril, p1) + _dot(tril, p2))
    carry_ref[0:1, :] = cum[tt - 1:tt, :]
    edge_rows = []
    for jb in range(tt // SUB):
        edge_rows += [cum[jb * SUB:jb * SUB + 1, :], cum[(jb + 1) * SUB - 1:(jb + 1) * SUB, :]]
    edge_rows.append(jnp.zeros((co_ref.shape[2] - len(edge_rows), LANES), F32))
    co_ref[0, 0] = jnp.concatenate(edge_rows, axis=0)
    c0, c1, c2 = _split3(cum * LOG2E)
    e = _dot(c0, sel_ref[0]) + _dot(c1, sel_ref[1]) + _dot(c2, sel_ref[2])
    ex_q = e[:, :LANES] + cq_ref[...]
    ek_all = e[:, LANES:]
    bd = bd_ref[...]
    low = lane < HEAD_DIM
    is_x = (lane >= X_LANE0) & (lane < X_LANE0 + 3)

    def normed(x_all, gain_ref, scale, pair):
        x = x_all[:, pair * LANES:(pair + 1) * LANES]
        ms = _dot((x * x).astype(BF16), bd)
        return x * lax.rsqrt(ms + EPS) * (gain_ref[:, pair * LANES:(pair + 1) * LANES] * scale)

    for pair in range(n_heads // 2):
        qn = normed(fq, qg_ref, HEAD_DIM ** -0.5 * LOG2E, pair)
        kn = normed(fk, kg_ref, 1.0, pair)
        for odd in range(2):
            h = 2 * pair + odd
            if odd:
                qh = pltpu.roll(qn, HEAD_DIM, axis=1)
                kh = pltpu.roll(kn, HEAD_DIM, axis=1)
            else:
                qh, kh = qn, kn
            ex_k = jnp.where(is_x, pltpu.roll(ek_all, (X_LANE0 - 16 * h) % LANES, axis=1),
                             ind_ref[h:h + 1, :])
            qo_ref[0, h] = jnp.where(low, qh, ex_q).astype(BF16)
            ko_ref[0, h] = jnp.where(low, kh, ex_k).astype(BF16)

    ones_rows = (lax.broadcasted_iota(jnp.int32, (V_ROWS - HEAD_DIM, SUB), 0) == 0).astype(BF16)
    for h in range(n_heads):
        for jb in range(tt // SUB):
            vo_ref[0, h, jb, :HEAD_DIM, :] = vt[h * HEAD_DIM:(h + 1) * HEAD_DIM,
                                                jb * SUB:(jb + 1) * SUB].astype(BF16)
            vo_ref[0, h, jb, HEAD_DIM:, :] = ones_rows


def _inproj0_kernel(x_ref, sc_ref, sh_ref, wg_ref, wf_ref, wvt_ref, ws_ref, bf_ref, qg_ref, kg_ref, tril_ref,
                    bd_ref, sel_ref, cq_ref, ind_ref, og_ref, os_ref, qo_ref, ko_ref, vo_ref, co_ref,
                    carry_ref, *, n_heads):
    @pl.when(pl.program_id(1) == 0)
    def _():
        carry_ref[...] = jnp.zeros_like(carry_ref)

    h = _norm_mod(x_ref[0], sc_ref[0], sh_ref[0]).astype(BF16)
    w = og_ref.shape[2] // 4
    sm = _dot(h, ws_ref[...])
    os_ref[0] = sm
    fq = _dot(h, wf_ref[:, :w])
    fk = _dot(h, wf_ref[:, w:])
    vt = _dot_nt(wvt_ref[...], h)
    _fox_prep_math(fq, fk, vt, sm, bf_ref, qg_ref, kg_ref, tril_ref, bd_ref, sel_ref, cq_ref, ind_ref,
                   qo_ref, ko_ref, vo_ref, co_ref, carry_ref, n_heads)
    for c0 in range(0, 4 * w, w):
        og_ref[0, :, c0:c0 + w] = _dot(h, wg_ref[:, c0:c0 + w]).astype(BF16)


def _fox_prep_consts(n_heads):
    sel = np.zeros((3, LANES, 2 * LANES), np.float32)
    cq = np.zeros((1, LANES), np.float32)
    ind = np.zeros((n_heads, LANES), np.float32)
    for h in range(n_heads):
        for r in range(3):
            sel[r, FF_LANE0 + h, C_LANE0 + 3 * h + r] = 1.0
            sel[r, FF_LANE0 + h, LANES + 16 * h + r] = -1.0
            ind[h, C_LANE0 + 3 * h + r] = 1.0
            cq[0, X_LANE0 + r] = 1.0
    return jnp.asarray(sel, BF16), jnp.asarray(cq), jnp.asarray(ind), _head_mean_matrix()


def _inproj0(x, sc, sh, w_gla, w_fox, w_vt, w_small, bf_pad, q_gain, k_gain, tm):
    bsz, s, d = x.shape
    w = q_gain.shape[1]
    n_heads = w // HEAD_DIM
    sel, cq, ind, bd = _fox_prep_consts(n_heads)
    tril = jnp.asarray(np.tril(np.ones((tm, tm), np.float32)), BF16)
    head_spec = pl.BlockSpec((1, n_heads, tm, LANES), lambda b, i: (b, 0, i, 0))
    head_sds = jax.ShapeDtypeStruct((bsz, n_heads, s, LANES), BF16)
    return pl.pallas_call(
        functools.partial(_inproj0_kernel, n_heads=n_heads),
        grid=(bsz, s // tm),
        in_specs=[pl.BlockSpec((1, tm, d), lambda b, i: (b, i, 0)),
                  pl.BlockSpec((1, 1, d), lambda b, i: (b, 0, 0)),
                  pl.BlockSpec((1, 1, d), lambda b, i: (b, 0, 0)),
                  _const_spec(w_gla.shape), _const_spec(w_fox.shape), _const_spec(w_vt.shape),
                  _const_spec(w_small.shape),
                  _const_spec((1, LANES)), _const_spec((1, w)), _const_spec((1, w)),
                  _const_spec((tm, tm)), _const_spec((LANES, LANES)),
                  _const_spec(sel.shape), _const_spec(cq.shape), _const_spec(ind.shape)],
        out_specs=[pl.BlockSpec((1, tm, 4 * w), lambda b, i: (b, i, 0)),
                   pl.BlockSpec((1, tm, LANES), lambda b, i: (b, i, 0)),
                   head_spec, head_spec,
                   pl.BlockSpec((1, n_heads, tm // SUB, V_ROWS, SUB), lambda b, i: (b, 0, i, 0, 0)),
                   pl.BlockSpec((1, 1, 8, LANES), lambda b, i: (b, i, 0, 0))],
        out_shape=[jax.ShapeDtypeStruct((bsz, s, 4 * w), BF16),
                   jax.ShapeDtypeStruct((bsz, s, LANES), F32),
                   head_sds, head_sds,
                   jax.ShapeDtypeStruct((bsz, n_heads, s // SUB, V_ROWS, SUB), BF16),
                   jax.ShapeDtypeStruct((bsz, s // tm, 8, LANES), F32)],
        scratch_shapes=[pltpu.VMEM((8, LANES), F32)],
        compiler_params=_cparams("parallel", "arbitrary"),
        name="inproj0",
    )(x, sc, sh, w_gla, w_fox, w_vt, w_small, bf_pad, q_gain, k_gain, tril, bd, sel, cq, ind)


ZERO_MARGIN = 160.0
LOOKAHEAD = 2


def _fox_attn_kernel(cq_ref, ce_ref, thr_ref, q_ref, k_ref, vt_ref, o_ref, m_ref, acc_ref, s_ref,
                     tri_ref, *, n_sub):
    b = pl.program_id(0)
    pair = pl.program_id(1)
    i = pl.program_id(2)
    row = lax.broadcasted_iota(jnp.int32, (SUB, SUB), 0)
    col = lax.broadcasted_iota(jnp.int32, (SUB, SUB), 1)
    tri_ref[...] = jnp.where(row <= col, 0.0, NEG_BIG)

    def blocks(r):
        a = i * n_sub + r
        return a, jnp.maximum(a - 1, 0)

    chains = [(r, hh) for r in range(n_sub) for hh in range(2)]

    def scores(r, hh):
        a, jp = blocks(r)
        q = q_ref[0, hh, r * SUB:(r + 1) * SUB, :]
        s_ref[r, hh, :SUB, :] = _dot_nt(k_ref[0, hh, pl.ds(pl.multiple_of(jp * SUB, SUB), SUB), :], q)
        s_ref[r, hh, SUB:, :] = _dot_nt(k_ref[0, hh, pl.ds(pl.multiple_of(a * SUB, SUB), SUB), :], q)

    def softmax_pv(r, hh):
        a, jp = blocks(r)
        no_prev = jnp.where(a == 0, NEG_BIG, 0.0)
        st = s_ref[r, hh, :SUB, :] + no_prev
        sd = s_ref[r, hh, SUB:, :] + tri_ref[...]
        m = jnp.maximum(jnp.max(st, axis=0, keepdims=True), jnp.max(sd, axis=0, keepdims=True))
        pt = jnp.exp2((st - m).astype(BF16))
        pd = jnp.exp2((sd - m).astype(BF16))
        acc_ref[r, hh] = _dot(vt_ref[0, hh, jp], pt) + _dot(vt_ref[0, hh, a], pd)
        m_ref[r, hh, 0:1, :] = m

    for n in range(len(chains) + LOOKAHEAD):
        if n < len(chains):
            scores(*chains[n])
        if n >= LOOKAHEAD:
            softmax_pv(*chains[n - LOOKAHEAD])

    def reaches(r, hh, j):
        h = 2 * pair + hh
        a = i * n_sub + r
        return (j >= 0) & (cq_ref[b, h, a] - ce_ref[b, h, jnp.maximum(j, 0)] >= thr_ref[h])

    any_more = functools.reduce(jnp.logical_or,
                                [reaches(r, hh, i * n_sub + r - 2) for r, hh in chains])

    @pl.when(any_more)
    def _():
        for r, hh in chains:
            q = q_ref[0, hh, r * SUB:(r + 1) * SUB, :]

            def body(c, hh=hh, q=q):
                j, m, acc = c
                kb = k_ref[0, hh, pl.ds(pl.multiple_of(j * SUB, SUB), SUB), :]
                s = _dot_nt(kb, q)
                m_new = jnp.maximum(m, jnp.max(s, axis=0, keepdims=True))
                p = jnp.exp2((s - m_new).astype(BF16))
                return j - 1, m_new, jnp.exp2(m - m_new) * acc + _dot(vt_ref[0, hh, j], p)

            _, m, acc = lax.while_loop(lambda c, r=r, hh=hh: reaches(r, hh, c[0]), body,
                                       (i * n_sub + r - 2, m_ref[r, hh, 0:1, :], acc_ref[r, hh]))
            acc_ref[r, hh] = acc
            m_ref[r, hh, 0:1, :] = m

    for r in range(n_sub):
        outs = [acc_ref[r, hh, :HEAD_DIM, :] / acc_ref[r, hh, HEAD_DIM:HEAD_DIM + 1, :]
                for hh in range(2)]
        o_ref[0, r * SUB:(r + 1) * SUB, :] = jnp.concatenate(outs, axis=0).T.astype(BF16)


def _fox_attn(cq, ce, thr, qp, kp, vt, tq):
    bsz, n_heads, s, _ = qp.shape
    nkb = s // SUB
    n_sub = tq // SUB
    grid_spec = pltpu.PrefetchScalarGridSpec(
        num_scalar_prefetch=3,
        grid=(bsz, n_heads // 2, s // tq),
        in_specs=[pl.BlockSpec((1, 2, tq, LANES), lambda b, p, i, *_: (b, p, i, 0)),
                  pl.BlockSpec((1, 2, s, LANES), lambda b, p, i, *_: (b, p, 0, 0)),
                  pl.BlockSpec((1, 2, nkb, V_ROWS, SUB), lambda b, p, i, *_: (b, p, 0, 0, 0))],
        out_specs=pl.BlockSpec((1, tq, LANES), lambda b, p, i, *_: (b, i, p)),
        scratch_shapes=[pltpu.VMEM((n_sub, 2, 8, SUB), F32),
                        pltpu.VMEM((n_sub, 2, V_ROWS, SUB), F32),
                        pltpu.VMEM((n_sub, 2, 2 * SUB, SUB), F32),
                        pltpu.VMEM((SUB, SUB), F32)])
    return pl.pallas_call(
        functools.partial(_fox_attn_kernel, n_sub=n_sub),
        grid_spec=grid_spec,
        out_shape=jax.ShapeDtypeStruct((bsz, s, n_heads * HEAD_DIM), BF16),
        compiler_params=_cparams("parallel", "parallel", "arbitrary"),
        name="fox_attn",
    )(cq, ce, thr, qp, kp, vt)


def _fox_skip_tables(edges, q_gain, k_gain, n_heads, blocks):
    bsz, nt = edges.shape[:2]
    c2 = edges[:, :, :2 * blocks, FF_LANE0:FF_LANE0 + n_heads] * LOG2E
    c2 = c2.reshape(bsz, nt * blocks, 2, n_heads)
    cq = c2[:, :, 0, :].transpose(0, 2, 1)
    ce = c2[:, :, 1, :].transpose(0, 2, 1)
    bound = (jnp.max(jnp.abs(q_gain), axis=-1) * jnp.max(jnp.abs(k_gain), axis=-1)
             * (HEAD_DIM ** 0.5) * LOG2E)
    thr = -(2.0 * 1.02 * bound + ZERO_MARGIN)
    return cq, ce, thr


def _s5_kernel(u_ref, kt_ref, min_ref, mout_ref, a_ref, d_ref, o_ref,
               t_ref, v_ref, vs_ref, xp_ref, *, bsz, n_col):
    L = S5_CHUNK
    W = S5_GROUP_WIDTH
    per = LANES // L
    nc = u_ref.shape[1]
    row = lax.broadcasted_iota(jnp.int32, (L, LANES), 0)
    lane = lax.broadcasted_iota(jnp.int32, (L, LANES), 1)
    causal = (lane % L) >= row

    tiles_per_chunk = n_col // LANES

    def build(chunk):
        for j in range(W):
            for ip in range(chunk * tiles_per_chunk, (chunk + 1) * tiles_per_chunk):
                tile = jnp.broadcast_to(kt_ref[0, j, ip:ip + 1, :], (L, LANES))
                tile = pltpu.roll(tile, 0, axis=1, stride=1, stride_axis=0)
                t_ref[j * L:(j + 1) * L, ip * LANES:(ip + 1) * LANES] = (
                    jnp.where(causal, tile, 0.0).astype(BF16))

    build(0)

    u = jnp.concatenate([u_ref[j] for j in range(W)], axis=-1)
    v = _dot(u, min_ref[0])
    v_ref[...] = v
    vs_ref[...] = pltpu.roll(v, S5_STATE, axis=1)
    a1 = a_ref[0, 0:1, :]
    a2 = a_ref[0, 1:2, :]

    n_steps = nc // bsz

    def scan(c, carry):
        x, xs = carry
        rows = pl.ds(c, bsz, stride=n_steps)
        xp_ref[rows, :] = x
        return (x * a1 + xs * a2 + v_ref[rows, :], xs * a1 - x * a2 + vs_ref[rows, :])

    zero = jnp.zeros((bsz, 2 * S5_STATE), F32)
    lax.fori_loop(0, n_steps, scan, (zero, zero), unroll=8)

    xp = xp_ref[...].astype(BF16)
    for c0 in range(0, W * L, n_col):
        cs = slice(c0, c0 + n_col)
        if c0 + n_col < W * L:
            build(c0 // n_col + 1)
        y = (_dot(u, t_ref[:, cs]) + _dot(xp, mout_ref[0, :, cs])
             + d_ref[0, :, cs] * u[:, cs].astype(F32))
        y = jax.nn.gelu(y).astype(BF16)
        for k in range(n_col // L):
            o_ref[c0 // L + k] = y[:, k * L:(k + 1) * L]


def _s5(u_t, ktab, m_in, m_out, a_pack, d_vec, bsz):
    chans, nc, L = u_t.shape
    W = S5_GROUP_WIDTH
    wl = W * L
    return pl.pallas_call(
        functools.partial(_s5_kernel, bsz=bsz, n_col=512),
        grid=(chans // W,),
        in_specs=[pl.BlockSpec((W, nc, L), lambda g: (g, 0, 0)),
                  pl.BlockSpec((1,) + ktab.shape[1:], lambda g: (g, 0, 0, 0)),
                  pl.BlockSpec((1, wl, 2 * S5_STATE), lambda g: (g, 0, 0)),
                  pl.BlockSpec((1, 2 * S5_STATE, wl), lambda g: (g, 0, 0)),
                  pl.BlockSpec((1, 2, 2 * S5_STATE), lambda g: (g, 0, 0)),
                  pl.BlockSpec((1, 1, wl), lambda g: (g, 0, 0))],
        out_specs=pl.BlockSpec((W, nc, L), lambda g: (g, 0, 0)),
        out_shape=jax.ShapeDtypeStruct((chans, nc, L), BF16),
        scratch_shapes=[pltpu.VMEM((wl, wl), BF16),
                        pltpu.VMEM((nc, 2 * S5_STATE), F32),
                        pltpu.VMEM((nc, 2 * S5_STATE), F32),
                        pltpu.VMEM((nc, 2 * S5_STATE), F32)],
        compiler_params=_cparams("parallel"),
        name="s5",
    )(u_t, ktab, m_in, m_out, a_pack, d_vec)


def _s5_tables(lam_re, lam_im, log_dt, b_re, b_im, c_re, c_im, d_skip):
    L = S5_CHUNK
    dt = jnp.exp(log_dt)[:, None]
    mag = jnp.exp(lam_re * dt)
    ang = lam_im * dt
    ar, ai = mag * jnp.cos(ang), mag * jnp.sin(ang)
    den = lam_re * lam_re + lam_im * lam_im
    cr = ((ar - 1.0) * lam_re + ai * lam_im) / den
    ci = (ai * lam_re - (ar - 1.0) * lam_im) / den
    bbr = cr[..., None] * b_re - ci[..., None] * b_im
    bbi = cr[..., None] * b_im + ci[..., None] * b_re
    pr, pi = jnp.ones_like(ar)[None], jnp.zeros_like(ar)[None]
    sr, si = ar, ai
    n = 1
    while n <= L:
        pr, pi = (jnp.concatenate([pr, pr * sr - pi * si], 0),
                  jnp.concatenate([pi, pr * si + pi * sr], 0))
        sr, si = sr * sr - si * si, 2.0 * sr * si
        n *= 2
    pr, pi = pr[:L + 1], pi[:L + 1]
    hp = lax.Precision.HIGHEST
    er = c_re[None] * pr[:L, :, None, :] - c_im[None] * pi[:L, :, None, :]
    ei = -(c_re[None] * pi[:L, :, None, :] + c_im[None] * pr[:L, :, None, :])
    ktab = (jnp.einsum('lgip,gpj->gjil', er, bbr, precision=hp)
            + jnp.einsum('lgip,gpj->gjil', ei, bbi, precision=hp))
    qr, qi = pr[:L][::-1], pi[:L][::-1]
    min_re = qr[:, :, :, None] * bbr[None] - qi[:, :, :, None] * bbi[None]
    min_im = qr[:, :, :, None] * bbi[None] + qi[:, :, :, None] * bbr[None]
    m_in = jnp.concatenate([min_re, min_im], axis=2)
    m_in = m_in.transpose(1, 3, 0, 2).reshape(m_in.shape[1], -1, 2 * S5_STATE)
    tr, ti = pr[1:L + 1], pi[1:L + 1]
    mo_re = c_re[None] * tr[:, :, None, :] - c_im[None] * ti[:, :, None, :]
    mo_im = -(c_re[None] * ti[:, :, None, :] + c_im[None] * tr[:, :, None, :])
    m_out = jnp.concatenate([mo_re, mo_im], axis=3)
    m_out = m_out.transpose(1, 3, 2, 0).reshape(m_out.shape[1], 2 * S5_STATE, -1)
    a_pack = jnp.stack([jnp.concatenate([pr[L], pr[L]], -1),
                        jnp.concatenate([-pi[L], pi[L]], -1)], axis=1)
    d_vec = jnp.repeat(d_skip, L, axis=-1)[:, None, :]
    groups, width = ktab.shape[:2]
    ktab = ktab.reshape(groups, width, width * L // LANES, LANES)
    return ktab, m_in.astype(BF16), m_out.astype(BF16), a_pack, d_vec


def _sgu_math(zu, zv, w_ref, bias_ref, lg_ref, lb_ref, o_ref):
    L = SGU_CHUNK
    row = lax.broadcasted_iota(jnp.int32, (L, L), 0)
    colm = lax.broadcasted_iota(jnp.int32, (L, L), 1)
    causal = colm <= row
    v = jax.nn.gelu(zv)
    mu = jnp.mean(v, axis=-1, keepdims=True)
    vc = v - mu
    var = jnp.mean(vc * vc, axis=-1, keepdims=True)
    vn = (vc * lax.rsqrt(var + EPS) * lg_ref[...] + lb_ref[...]).astype(BF16)
    gw = vn.shape[1] // SGU_GROUPS
    ws = [jnp.where(causal, w_ref[g], 0.0).astype(BF16) for g in range(SGU_GROUPS)]
    for c in range(vn.shape[0] // L):
        rows = slice(c * L, (c + 1) * L)
        mixed = jnp.concatenate(
            [_dot(ws[g], vn[rows, g * gw:(g + 1) * gw]) for g in range(SGU_GROUPS)], axis=-1)
        o_ref[0, rows, :] = (jax.nn.gelu(zu[rows]) * (mixed + bias_ref[...])).astype(BF16)


def _inproj_sgu_kernel(x_ref, sc_ref, sh_ref, wm_ref, wt_ref, w_ref, bias_ref, lg_ref, lb_ref,
                       o_ref, ot_ref):
    h = _norm_mod(x_ref[0], sc_ref[0], sh_ref[0]).astype(BF16)
    half = o_ref.shape[2]
    zv = _dot(h, wm_ref[:, half:])
    zu = _dot(h, wm_ref[:, :half])
    _sgu_math(zu, zv, w_ref, bias_ref, lg_ref, lb_ref, o_ref)
    ot_ref[...] = _dot_nt(wt_ref[...], h).astype(BF16)


def _inproj_sgu(x, sc, sh, w_sgu, w_t, w_s, bias_full, ln_gain, ln_bias, tm):
    bsz, s, d = x.shape
    half = w_t.shape[0]
    nt = s // tm
    return pl.pallas_call(
        _inproj_sgu_kernel,
        grid=(bsz, nt),
        in_specs=[pl.BlockSpec((1, tm, d), lambda b, i: (b, i, 0)),
                  pl.BlockSpec((1, 1, d), lambda b, i: (b, 0, 0)),
                  pl.BlockSpec((1, 1, d), lambda b, i: (b, 0, 0)),
                  _const_spec(w_sgu.shape), _const_spec(w_t.shape),
                  _const_spec(w_s.shape), _const_spec(bias_full.shape),
                  _const_spec((1, half)), _const_spec((1, half))],
        out_specs=[pl.BlockSpec((1, tm, half), lambda b, i: (b, i, 0)),
                   pl.BlockSpec((half, tm), lambda b, i: (0, b * nt + i))],
        out_shape=[jax.ShapeDtypeStruct((bsz, s, half), BF16),
                   jax.ShapeDtypeStruct((half, bsz * s), BF16)],
        compiler_params=_cparams("parallel", "parallel"),
        name="inproj_sgu",
    )(x, sc, sh, w_sgu, w_t, w_s, bias_full, ln_gain, ln_bias)


def _tail_kernel(x_ref, a_ref, b_ref, wo_ref, g1_ref, sc_ref, sh_ref, g2_ref, w1_ref, w2_ref,
                 *rest, f_chunk, with_glu):
    if with_glu:
        wg_ref, bg_ref, o_ref = rest
        af = a_ref[...].astype(F32).T
        a = af.astype(BF16)
        a = (af * jax.nn.sigmoid(_dot(a, wg_ref[...]) + bg_ref[...])).astype(BF16)
    else:
        (o_ref,) = rest
        a = a_ref[0]
    half = a.shape[1]
    y = _dot(a, wo_ref[:half, :]) + _dot(b_ref[0], wo_ref[half:, :])
    x1 = x_ref[0] + g1_ref[0] * y
    h = _norm_mod(x1, sc_ref[0], sh_ref[0]).astype(BF16)
    f = w1_ref.shape[1]
    acc = None
    for f0 in range(0, f, f_chunk):
        hid = jnp.maximum(_dot(h, w1_ref[:, f0:f0 + f_chunk]), 0.0)
        part = _dot((hid * hid).astype(BF16), w2_ref[f0:f0 + f_chunk, :])
        acc = part if acc is None else acc + part
    o_ref[0] = x1 + g2_ref[0] * acc


def _tail(x, mix_a, mix_b, w_out, g1, sc, sh, g2, w1, w2, glu, tm):
    bsz, s, d = x.shape
    half = mix_b.shape[-1]
    f = w1.shape[1]
    nt = s // tm
    row = pl.BlockSpec((1, 1, d), lambda b, i: (b, 0, 0))
    if glu is not None:
        a_spec = pl.BlockSpec((half, tm), lambda b, i: (0, b * nt + i))
    else:
        a_spec = pl.BlockSpec((1, tm, half), lambda b, i: (b, i, 0))
    in_specs = [pl.BlockSpec((1, tm, d), lambda b, i: (b, i, 0)),
                a_spec,
                pl.BlockSpec((1, tm, half), lambda b, i: (b, i, 0)),
                _const_spec((d, d)), row, row, row, row,
                _const_spec((d, f)), _const_spec((f, d))]
    args = [x, mix_a, mix_b, w_out, g1, sc, sh, g2, w1, w2]
    if glu is not None:
        in_specs += [_const_spec((half, half)), _const_spec((1, half))]
        args += list(glu)
    return pl.pallas_call(
        functools.partial(_tail_kernel, f_chunk=1024, with_glu=glu is not None),
        grid=(bsz, nt),
        in_specs=in_specs,
        out_specs=pl.BlockSpec((1, tm, d), lambda b, i: (b, i, 0)),
        out_shape=jax.ShapeDtypeStruct((bsz, s, d), F32),
        compiler_params=_cparams("parallel", "parallel"),
        name="tail",
    )(*args)


def _tile(s, want):
    t = min(s, want)
    assert s % t == 0, (s, t)
    return t


def kernel(x, c, ada_w, ada_b, even_w_in, even_w_out, gla_w_lr, gla_b_lr, gla_gain, fox_b_f, fox_q_gain, fox_k_gain, odd_w_in, odd_w_out, s5_lam_re, s5_lam_im, s5_log_dt, s5_b_re, s5_b_im, s5_c_re, s5_c_im, s5_d, s5_w_glu, s5_b_glu, sgu_ln_gain, sgu_ln_bias, sgu_w_s, sgu_b_s, mlp_w1, mlp_w2):
    bsz, s, d = x.shape
    half = d // 2
    n_heads = half // HEAD_DIM
    tm = _tile(s, 512)

    mod = _ada_mod(c, ada_w, ada_b)

    def mods(layer):
        return [m[:, None, :] for m in jnp.split(mod[layer], 6, axis=-1)]

    sh1, sc1, g1, sh2, sc2, g2 = mods(0)
    w_in = even_w_in[0]
    o_lr = 4 * half
    o_fq = o_lr + GLA_RANK
    o_ff = o_fq + 3 * half
    w_small = jnp.concatenate(
        [w_in[:, o_lr:o_fq], w_in[:, o_ff:],
         jnp.zeros((d, LANES - GLA_RANK - n_heads), w_in.dtype)], axis=1).astype(BF16)
    bf_pad = jnp.zeros((1, LANES), F32).at[0, FF_LANE0:FF_LANE0 + n_heads].set(fox_b_f[0])
    proj, small, qp, kp, vt, edges = _inproj0(
        x, sc1, sh1, w_in[:, :o_lr].astype(BF16), w_in[:, o_fq:o_fq + 2 * half].astype(BF16),
        w_in[:, o_fq + 2 * half:o_ff].T.astype(BF16), w_small, bf_pad,
        fox_q_gain[0].reshape(1, half), fox_k_gain[0].reshape(1, half), tm)

    wlr_pad = jnp.concatenate([gla_w_lr[0], jnp.zeros((LANES - GLA_RANK, half), F32)],
                              axis=0).astype(BF16)
    o_gla = _gla(proj, small, wlr_pad, gla_b_lr[0][None, :], gla_gain[0].reshape(1, half), tm)

    cq, ce, thr = _fox_skip_tables(edges, fox_q_gain[0], fox_k_gain[0], n_heads, tm // SUB)
    o_fox = _fox_attn(cq, ce, thr, qp, kp, vt, _tile(s, 2048))

    x = _tail(x, o_gla, o_fox, even_w_out[0].astype(BF16), g1, sc2, sh2, g2,
              mlp_w1[0].astype(BF16), mlp_w2[0].astype(BF16), None, tm)

    sh1, sc1, g1, sh2, sc2, g2 = mods(1)
    w_in = odd_w_in[0].astype(BF16)
    bias_full = jnp.repeat(sgu_b_s[0].T, half // SGU_GROUPS, axis=1)
    y_sgu, u_t = _inproj_sgu(x, sc1, sh1, w_in[:, half:], w_in[:, :half].T, sgu_w_s[0], bias_full,
                             sgu_ln_gain[0][None, :], sgu_ln_bias[0][None, :], tm)
    ktab, m_in, m_out, a_pack, d_vec = _s5_tables(
        s5_lam_re[0], s5_lam_im[0], s5_log_dt[0], s5_b_re[0], s5_b_im[0],
        s5_c_re[0], s5_c_im[0], s5_d[0])
    n_rows = bsz * s // S5_CHUNK
    y_t = _s5(u_t.reshape(half, n_rows, S5_CHUNK), ktab, m_in, m_out, a_pack, d_vec, bsz)
    y_t = y_t.reshape(half, bsz * s)

    x = _tail(x, y_t, y_sgu, odd_w_out[0].astype(BF16), g1, sc2, sh2, g2,
              mlp_w1[1].astype(BF16), mlp_w2[1].astype(BF16),
              (s5_w_glu[0].astype(BF16), s5_b_glu[0][None, :]), tm)
    return x
```

```python
import functools

import numpy as np
import jax
import jax.numpy as jnp
from jax import lax
from jax.experimental import pallas as pl
from jax.experimental.pallas import tpu as pltpu

F32 = jnp.float32
BF16 = jnp.bfloat16

EPS = 1e-6
HEAD_DIM = 64
LANES = 128
GLA_RANK = 16
GLA_TAU = 16.0
GLA_CHUNK = 64
S5_GROUP_WIDTH = 16
S5_STATE = 64
S5_CHUNK = 128
SGU_GROUPS = 8
SGU_CHUNK = 128
VMEM_LIMIT_BYTES = 56 * 1024 * 1024
NEG_BIG = -1e30
LOG2E = 1.4426950408889634


def _cparams(*sem):
    return pltpu.CompilerParams(dimension_semantics=sem, vmem_limit_bytes=VMEM_LIMIT_BYTES)


def _const_spec(shape):
    return pl.BlockSpec(shape, lambda *_: (0,) * len(shape), pipeline_mode=pl.Buffered(1))


def _dot(a, b):
    return jnp.dot(a, b, preferred_element_type=F32)


def _dot_nt(a, b):
    return lax.dot_general(a, b, (((1,), (1,)), ((), ())), preferred_element_type=F32)


def _dot_tn(a, b):
    return lax.dot_general(a, b, (((0,), (0,)), ((), ())), preferred_element_type=F32)


def _split2(x):
    hi = x.astype(BF16)
    lo = (x - hi.astype(F32)).astype(BF16)
    return hi, lo


def _split3(x):
    hi = x.astype(BF16)
    r = x - hi.astype(F32)
    mid = r.astype(BF16)
    lo = (r - mid.astype(F32)).astype(BF16)
    return hi, mid, lo


def _log_sigmoid(z):
    return jnp.minimum(z, 0.0) - jnp.log(1.0 + jnp.exp(-jnp.abs(z)))


def _head_mean_matrix():
    return jnp.asarray(np.kron(np.eye(LANES // HEAD_DIM, dtype=np.float32),
                               np.full((HEAD_DIM, HEAD_DIM), 1.0 / HEAD_DIM, np.float32)), BF16)


def _norm_mod(x, sc, sh):
    ms = jnp.mean(x * x, axis=-1, keepdims=True)
    return x * lax.rsqrt(ms + EPS) * (1.0 + sc) + sh


def _ada_kernel(c_ref, w_ref, b_ref, o_ref):
    c = c_ref[...]
    ca = c * jax.nn.sigmoid(c)
    o_ref[0] = _dot(ca, w_ref[0]) + b_ref[0]


def _ada_mod(c, ada_w, ada_b):
    depth, d, n = ada_w.shape
    bsz = c.shape[0]
    tn = n // 4
    return pl.pallas_call(
        _ada_kernel,
        grid=(depth, n // tn),
        in_specs=[pl.BlockSpec((bsz, d), lambda l, j: (0, 0)),
                  pl.BlockSpec((1, d, tn), lambda l, j: (l, 0, j)),
                  pl.BlockSpec((1, 1, tn), lambda l, j: (l, 0, j))],
        out_specs=pl.BlockSpec((1, bsz, tn), lambda l, j: (l, 0, j)),
        out_shape=jax.ShapeDtypeStruct((depth, bsz, n), F32),
        compiler_params=_cparams("parallel", "parallel"),
        name="ada_mod",
    )(c, ada_w, ada_b.reshape(depth, 1, n))


def _gla_kernel(q_ref, k_ref, v_ref, g_ref, sm_ref, wlr_ref, blr_ref, gain_ref, tril_ref, bd_ref,
                o_ref, state_ref, a_ref, kv_ref, st_ref, qin_ref, ebl_ref, o_acc_ref,
                *, n_heads, n_chunks):
    C = GLA_CHUNK
    mid = C // 2 - 1

    @pl.when(pl.program_id(1) == 0)
    def _():
        state_ref[...] = jnp.zeros_like(state_ref)

    row = lax.broadcasted_iota(jnp.int32, (C, C), 0)
    col = lax.broadcasted_iota(jnp.int32, (C, C), 1)
    causal = col <= row
    z = _dot(sm_ref[0].astype(BF16), wlr_ref[...]) + blr_ref[...]
    lc = (_log_sigmoid(z) * (1.0 / GLA_TAU)).astype(BF16)
    bc_all = _dot(tril_ref[...], lc)
    heads = [slice(h * HEAD_DIM, (h + 1) * HEAD_DIM) for h in range(n_heads)]

    for c in range(n_chunks):
        rows = slice(c * C, (c + 1) * C)
        bc = bc_all[rows]
        m = bc[mid:mid + 1, :]
        bl = bc[C - 1:C, :]
        e = jnp.exp(bc - m)
        qt = q_ref[0, rows, :].astype(F32) * (HEAD_DIM ** -0.5) * e
        kt = k_ref[0, rows, :].astype(F32) * (1.0 / e)
        qin_ref[rows, :] = (qt * jnp.exp(m)).astype(BF16)
        kst = (kt * jnp.exp(bl - m)).astype(BF16)
        qt = qt.astype(BF16)
        kt = kt.astype(BF16)
        ebl_ref[c:c + 1, :] = jnp.exp(bl)
        v = v_ref[0, rows, :]
        for h, sl in enumerate(heads):
            a_ref[c, h] = jnp.where(causal, _dot_nt(qt[:, sl], kt[:, sl]), 0.0).astype(BF16)
            kv_ref[c, h] = _dot_tn(v[:, sl], kst[:, sl])

    for h, sl in enumerate(heads):
        st = state_ref[h]
        for c in range(n_chunks):
            st_ref[c, h] = st.astype(BF16)
            st = st * ebl_ref[c:c + 1, sl] + kv_ref[c, h]
        state_ref[h] = st

    for c in range(n_chunks):
        rows = slice(c * C, (c + 1) * C)
        for h, sl in enumerate(heads):
            o_acc_ref[rows, sl] = (_dot(a_ref[c, h], v_ref[0, rows, sl])
                                   + _dot_nt(qin_ref[rows, sl], st_ref[c, h]))

    bd = bd_ref[...]
    for p in range(n_heads * HEAD_DIM // LANES):
        cols = slice(p * LANES, (p + 1) * LANES)
        o = o_acc_ref[:, cols]
        ms = _dot((o * o).astype(BF16), bd)
        g = g_ref[0, :, cols].astype(F32)
        o_ref[0, :, cols] = (o * lax.rsqrt(ms + EPS) * gain_ref[:, cols]
                             * (g * jax.nn.sigmoid(g))).astype(BF16)


def _gla(proj, small, wlr_pad, blr, gain, tt):
    bsz, s, _ = proj.shape
    w = wlr_pad.shape[1]
    n_heads = w // HEAD_DIM

    def col(j):
        return pl.BlockSpec((1, tt, w), lambda b, i, j=j: (b, i, j))

    n_chunks = tt // GLA_CHUNK
    tril = jnp.asarray(np.kron(np.eye(n_chunks, dtype=np.float32),
                               np.tril(np.ones((GLA_CHUNK, GLA_CHUNK), np.float32))), BF16)
    bd = _head_mean_matrix()
    return pl.pallas_call(
        functools.partial(_gla_kernel, n_heads=n_heads, n_chunks=n_chunks),
        grid=(bsz, s // tt),
        in_specs=[col(0), col(1), col(2), col(3),
                  pl.BlockSpec((1, tt, LANES), lambda b, i: (b, i, 0)),
                  _const_spec((LANES, w)), _const_spec((1, w)), _const_spec((1, w)),
                  _const_spec((tt, tt)), _const_spec((LANES, LANES))],
        out_specs=pl.BlockSpec((1, tt, w), lambda b, i: (b, i, 0)),
        out_shape=jax.ShapeDtypeStruct((bsz, s, w), BF16),
        scratch_shapes=[pltpu.VMEM((n_heads, HEAD_DIM, HEAD_DIM), F32),
                        pltpu.VMEM((n_chunks, n_heads, GLA_CHUNK, GLA_CHUNK), BF16),
                        pltpu.VMEM((n_chunks, n_heads, HEAD_DIM, HEAD_DIM), F32),
                        pltpu.VMEM((n_chunks, n_heads, HEAD_DIM, HEAD_DIM), BF16),
                        pltpu.VMEM((tt, w), BF16),
                        pltpu.VMEM((n_chunks, w), F32),
                        pltpu.VMEM((tt, w), F32)],
        compiler_params=_cparams("parallel", "arbitrary"),
        name="gla",
    )(proj, proj, proj, proj, small, wlr_pad, blr, gain, tril, bd)


FF_LANE0 = GLA_RANK
C_LANE0 = HEAD_DIM
X_LANE0 = 96
SUB = 256
V_ROWS = HEAD_DIM + 16


def _fox_prep_math(fq, fk, vt, sm, bf_ref, qg_ref, kg_ref, tril_ref, bd_ref, sel_ref, cq_ref, ind_ref,
                   qo_ref, ko_ref, vo_ref, co_ref, carry_ref, n_heads):
    tt = sm.shape[0]
    lane = lax.broadcasted_iota(jnp.int32, (tt, LANES), 1)
    is_ff = (lane >= FF_LANE0) & (lane < FF_LANE0 + n_heads)
    ls = jnp.where(is_ff, _log_sigmoid(sm + bf_ref[...]), 0.0)
    tril = tril_ref[...]
    p0, p1, p2 = _split3(ls)
    cum = carry_ref[0:1, :] + (_dot(tril, p0) + _dot(tril, p1) + _dot(tril, p2))
    carry_ref[0:1, :] = cum[tt - 1:tt, :]
    edge_rows = []
    for jb in range(tt // SUB):
        edge_rows += [cum[jb * SUB:jb * SUB + 1, :], cum[(jb + 1) * SUB - 1:(jb + 1) * SUB, :]]
    edge_rows.append(jnp.zeros((co_ref.shape[2] - len(edge_rows), LANES), F32))
    co_ref[0, 0] = jnp.concatenate(edge_rows, axis=0)
    c0, c1, c2 = _split3(cum * LOG2E)
    e = _dot(c0, sel_ref[0]) + _dot(c1, sel_ref[1]) + _dot(c2, sel_ref[2])
    ex_q = e[:, :LANES] + cq_ref[...]
    ek_all = e[:, LANES:]
    bd = bd_ref[...]
    low = lane < HEAD_DIM
    is_x = (lane >= X_LANE0) & (lane < X_LANE0 + 3)

    def normed(x_all, gain_ref, scale, pair):
        x = x_all[:, pair * LANES:(pair + 1) * LANES]
        ms = _dot((x * x).astype(BF16), bd)
        return x * lax.rsqrt(ms + EPS) * (gain_ref[:, pair * LANES:(pair + 1) * LANES] * scale)

    for pair in range(n_heads // 2):
        qn = normed(fq, qg_ref, HEAD_DIM ** -0.5 * LOG2E, pair)
        kn = normed(fk, kg_ref, 1.0, pair)
        for odd in range(2):
            h = 2 * pair + odd
            if odd:
                qh = pltpu.roll(qn, HEAD_DIM, axis=1)
                kh = pltpu.roll(kn, HEAD_DIM, axis=1)
            else:
                qh, kh = qn, kn
            ex_k = jnp.where(is_x, pltpu.roll(ek_all, (X_LANE0 - 16 * h) % LANES, axis=1),
                             ind_ref[h:h + 1, :])
            qo_ref[0, h] = jnp.where(low, qh, ex_q).astype(BF16)
            ko_ref[0, h] = jnp.where(low, kh, ex_k).astype(BF16)

    ones_rows = (lax.broadcasted_iota(jnp.int32, (V_ROWS - HEAD_DIM, SUB), 0) == 0).astype(BF16)
    for h in range(n_heads):
        for jb in range(tt // SUB):
            vo_ref[0, h, jb, :HEAD_DIM, :] = vt[h * HEAD_DIM:(h + 1) * HEAD_DIM,
                                                jb * SUB:(jb + 1) * SUB].astype(BF16)
            vo_ref[0, h, jb, HEAD_DIM:, :] = ones_rows


def _inproj0_kernel(x_ref, sc_ref, sh_ref, wg_ref, wf_ref, wvt_ref, ws_ref, bf_ref, qg_ref, kg_ref, tril_ref,
                    bd_ref, sel_ref, cq_ref, ind_ref, og_ref, os_ref, qo_ref, ko_ref, vo_ref, co_ref,
                    carry_ref, *, n_heads):
    @pl.when(pl.program_id(1) == 0)
    def _():
        carry_ref[...] = jnp.zeros_like(carry_ref)

    h = _norm_mod(x_ref[0], sc_ref[0], sh_ref[0]).astype(BF16)
    w = og_ref.shape[2] // 4
    sm = _dot(h, ws_ref[...])
    os_ref[0] = sm
    fq = _dot(h, wf_ref[:, :w])
    fk = _dot(h, wf_ref[:, w:])
    vt = _dot_nt(wvt_ref[...], h)
    _fox_prep_math(fq, fk, vt, sm, bf_ref, qg_ref, kg_ref, tril_ref, bd_ref, sel_ref, cq_ref, ind_ref,
                   qo_ref, ko_ref, vo_ref, co_ref, carry_ref, n_heads)
    for c0 in range(0, 4 * w, w):
        og_ref[0, :, c0:c0 + w] = _dot(h, wg_ref[:, c0:c0 + w]).astype(BF16)


def _fox_prep_consts(n_heads):
    sel = np.zeros((3, LANES, 2 * LANES), np.float32)
    cq = np.zeros((1, LANES), np.float32)
    ind = np.zeros((n_heads, LANES), np.float32)
    for h in range(n_heads):
        for r in range(3):
            sel[r, FF_LANE0 + h, C_LANE0 + 3 * h + r] = 1.0
            sel[r, FF_LANE0 + h, LANES + 16 * h + r] = -1.0
            ind[h, C_LANE0 + 3 * h + r] = 1.0
            cq[0, X_LANE0 + r] = 1.0
    return jnp.asarray(sel, BF16), jnp.asarray(cq), jnp.asarray(ind), _head_mean_matrix()


def _inproj0(x, sc, sh, w_gla, w_fox, w_vt, w_small, bf_pad, q_gain, k_gain, tm):
    bsz, s, d = x.shape
    w = q_gain.shape[1]
    n_heads = w // HEAD_DIM
    sel, cq, ind, bd = _fox_prep_consts(n_heads)
    tril = jnp.asarray(np.tril(np.ones((tm, tm), np.float32)), BF16)
    head_spec = pl.BlockSpec((1, n_heads, tm, LANES), lambda b, i: (b, 0, i, 0))
    head_sds = jax.ShapeDtypeStruct((bsz, n_heads, s, LANES), BF16)
    return pl.pallas_call(
        functools.partial(_inproj0_kernel, n_heads=n_heads),
        grid=(bsz, s // tm),
        in_specs=[pl.BlockSpec((1, tm, d), lambda b, i: (b, i, 0)),
                  pl.BlockSpec((1, 1, d), lambda b, i: (b, 0, 0)),
                  pl.BlockSpec((1, 1, d), lambda b, i: (b, 0, 0)),
                  _const_spec(w_gla.shape), _const_spec(w_fox.shape), _const_spec(w_vt.shape),
                  _const_spec(w_small.shape),
                  _const_spec((1, LANES)), _const_spec((1, w)), _const_spec((1, w)),
                  _const_spec((tm, tm)), _const_spec((LANES, LANES)),
                  _const_spec(sel.shape), _const_spec(cq.shape), _const_spec(ind.shape)],
        out_specs=[pl.BlockSpec((1, tm, 4 * w), lambda b, i: (b, i, 0)),
                   pl.BlockSpec((1, tm, LANES), lambda b, i: (b, i, 0)),
                   head_spec, head_spec,
                   pl.BlockSpec((1, n_heads, tm // SUB, V_ROWS, SUB), lambda b, i: (b, 0, i, 0, 0)),
                   pl.BlockSpec((1, 1, 8, LANES), lambda b, i: (b, i, 0, 0))],
        out_shape=[jax.ShapeDtypeStruct((bsz, s, 4 * w), BF16),
                   jax.ShapeDtypeStruct((bsz, s, LANES), F32),
                   head_sds, head_sds,
                   jax.ShapeDtypeStruct((bsz, n_heads, s // SUB, V_ROWS, SUB), BF16),
                   jax.ShapeDtypeStruct((bsz, s // tm, 8, LANES), F32)],
        scratch_shapes=[pltpu.VMEM((8, LANES), F32)],
        compiler_params=_cparams("parallel", "arbitrary"),
        name="inproj0",
    )(x, sc, sh, w_gla, w_fox, w_vt, w_small, bf_pad, q_gain, k_gain, tril, bd, sel, cq, ind)


ZERO_MARGIN = 160.0
LOOKAHEAD = 4


def _fox_attn_kernel(cq_ref, ce_ref, thr_ref, q_ref, k_ref, vt_ref, o_ref, m_ref, acc_ref, s_ref,
                     tri_ref, *, n_sub):
    b = pl.program_id(0)
    pair = pl.program_id(1)
    i = pl.program_id(2)
    row = lax.broadcasted_iota(jnp.int32, (SUB, SUB), 0)
    col = lax.broadcasted_iota(jnp.int32, (SUB, SUB), 1)
    tri_ref[...] = jnp.where(row <= col, 0.0, NEG_BIG)

    def blocks(r):
        a = i * n_sub + r
        return a, jnp.maximum(a - 1, 0)

    chains = [(r, hh) for r in range(n_sub) for hh in range(2)]

    def scores(r, hh):
        a, jp = blocks(r)
        q = q_ref[0, hh, r * SUB:(r + 1) * SUB, :]
        s_ref[r, hh, :SUB, :] = _dot_nt(k_ref[0, hh, pl.ds(pl.multiple_of(jp * SUB, SUB), SUB), :], q)
        s_ref[r, hh, SUB:, :] = _dot_nt(k_ref[0, hh, pl.ds(pl.multiple_of(a * SUB, SUB), SUB), :], q)

    def softmax_pv(r, hh):
        a, jp = blocks(r)
        no_prev = jnp.where(a == 0, NEG_BIG, 0.0)
        st = s_ref[r, hh, :SUB, :] + no_prev
        sd = s_ref[r, hh, SUB:, :] + tri_ref[...]
        m = jnp.maximum(jnp.max(st, axis=0, keepdims=True), jnp.max(sd, axis=0, keepdims=True))
        pt = jnp.exp2((st - m).astype(BF16))
        pd = jnp.exp2((sd - m).astype(BF16))
        acc_ref[r, hh] = _dot(vt_ref[0, hh, jp], pt) + _dot(vt_ref[0, hh, a], pd)
        m_ref[r, hh, 0:1, :] = m

    for n in range(len(chains) + LOOKAHEAD):
        if n < len(chains):
            scores(*chains[n])
        if n >= LOOKAHEAD:
            softmax_pv(*chains[n - LOOKAHEAD])

    def reaches(r, hh, j):
        h = 2 * pair + hh
        a = i * n_sub + r
        return (j >= 0) & (cq_ref[b, h, a] - ce_ref[b, h, jnp.maximum(j, 0)] >= thr_ref[h])

    any_more = functools.reduce(jnp.logical_or,
                                [reaches(r, hh, i * n_sub + r - 2) for r, hh in chains])

    @pl.when(any_more)
    def _():
        for r, hh in chains:
            q = q_ref[0, hh, r * SUB:(r + 1) * SUB, :]

            def body(c, hh=hh, q=q):
                j, m, acc = c
                kb = k_ref[0, hh, pl.ds(pl.multiple_of(j * SUB, SUB), SUB), :]
                s = _dot_nt(kb, q)
                m_new = jnp.maximum(m, jnp.max(s, axis=0, keepdims=True))
                p = jnp.exp2((s - m_new).astype(BF16))
                return j - 1, m_new, jnp.exp2(m - m_new) * acc + _dot(vt_ref[0, hh, j], p)

            _, m, acc = lax.while_loop(lambda c, r=r, hh=hh: reaches(r, hh, c[0]), body,
                                       (i * n_sub + r - 2, m_ref[r, hh, 0:1, :], acc_ref[r, hh]))
            acc_ref[r, hh] = acc
            m_ref[r, hh, 0:1, :] = m

    for r in range(n_sub):
        outs = [acc_ref[r, hh, :HEAD_DIM, :] / acc_ref[r, hh, HEAD_DIM:HEAD_DIM + 1, :]
                for hh in range(2)]
        o_ref[0, r * SUB:(r + 1) * SUB, :] = jnp.concatenate(outs, axis=0).T.astype(BF16)


def _fox_attn(cq, ce, thr, qp, kp, vt, tq):
    bsz, n_heads, s, _ = qp.shape
    nkb = s // SUB
    n_sub = tq // SUB
    grid_spec = pltpu.PrefetchScalarGridSpec(
        num_scalar_prefetch=3,
        grid=(bsz, n_heads // 2, s // tq),
        in_specs=[pl.BlockSpec((1, 2, tq, LANES), lambda b, p, i, *_: (b, p, i, 0)),
                  pl.BlockSpec((1, 2, s, LANES), lambda b, p, i, *_: (b, p, 0, 0)),
                  pl.BlockSpec((1, 2, nkb, V_ROWS, SUB), lambda b, p, i, *_: (b, p, 0, 0, 0))],
        out_specs=pl.BlockSpec((1, tq, LANES), lambda b, p, i, *_: (b, i, p)),
        scratch_shapes=[pltpu.VMEM((n_sub, 2, 8, SUB), F32),
                        pltpu.VMEM((n_sub, 2, V_ROWS, SUB), F32),
                        pltpu.VMEM((n_sub, 2, 2 * SUB, SUB), F32),
                        pltpu.VMEM((SUB, SUB), F32)])
    return pl.pallas_call(
        functools.partial(_fox_attn_kernel, n_sub=n_sub),
        grid_spec=grid_spec,
        out_shape=jax.ShapeDtypeStruct((bsz, s, n_heads * HEAD_DIM), BF16),
        compiler_params=_cparams("parallel", "parallel", "arbitrary"),
        name="fox_attn",
    )(cq, ce, thr, qp, kp, vt)


def _fox_skip_tables(edges, q_gain, k_gain, n_heads, blocks):
    bsz, nt = edges.shape[:2]
    c2 = edges[:, :, :2 * blocks, FF_LANE0:FF_LANE0 + n_heads] * LOG2E
    c2 = c2.reshape(bsz, nt * blocks, 2, n_heads)
    cq = c2[:, :, 0, :].transpose(0, 2, 1)
    ce = c2[:, :, 1, :].transpose(0, 2, 1)
    bound = (jnp.max(jnp.abs(q_gain), axis=-1) * jnp.max(jnp.abs(k_gain), axis=-1)
             * (HEAD_DIM ** 0.5) * LOG2E)
    thr = -(2.0 * 1.02 * bound + ZERO_MARGIN)
    return cq, ce, thr


def _s5_kernel(u_ref, kt_ref, min_ref, mout_ref, a_ref, d_ref, o_ref,
               t_ref, v_ref, vs_ref, xp_ref, *, bsz, n_col):
    L = S5_CHUNK
    W = S5_GROUP_WIDTH
    per = LANES // L
    nc = u_ref.shape[1] // L
    row = lax.broadcasted_iota(jnp.int32, (L, LANES), 0)
    lane = lax.broadcasted_iota(jnp.int32, (L, LANES), 1)
    causal = (lane % L) >= row

    tiles_per_chunk = n_col // LANES

    def build(chunk):
        for j in range(W):
            for ip in range(chunk * tiles_per_chunk, (chunk + 1) * tiles_per_chunk):
                tile = jnp.broadcast_to(kt_ref[0, j, ip:ip + 1, :], (L, LANES))
                tile = pltpu.roll(tile, 0, axis=1, stride=1, stride_axis=0)
                t_ref[j * L:(j + 1) * L, ip * LANES:(ip + 1) * LANES] = (
                    jnp.where(causal, tile, 0.0).astype(BF16))

    build(0)

    u = jnp.concatenate([u_ref[j:j + 1, :].reshape(nc, L) for j in range(W)], axis=-1)
    v = _dot(u, min_ref[0])
    v_ref[...] = v
    vs_ref[...] = pltpu.roll(v, S5_STATE, axis=1)
    a1 = a_ref[0, 0:1, :]
    a2 = a_ref[0, 1:2, :]

    n_steps = nc // bsz

    def scan(c, carry):
        x, xs = carry
        rows = pl.ds(c, bsz, stride=n_steps)
        xp_ref[rows, :] = x
        return (x * a1 + xs * a2 + v_ref[rows, :], xs * a1 - x * a2 + vs_ref[rows, :])

    zero = jnp.zeros((bsz, 2 * S5_STATE), F32)
    lax.fori_loop(0, n_steps, scan, (zero, zero), unroll=8)

    xp = xp_ref[...].astype(BF16)
    for c0 in range(0, W * L, n_col):
        cs = slice(c0, c0 + n_col)
        if c0 + n_col < W * L:
            build(c0 // n_col + 1)
        y = (_dot(u, t_ref[:, cs]) + _dot(xp, mout_ref[0, :, cs])
             + d_ref[0, :, cs] * u[:, cs].astype(F32))
        y = jax.nn.gelu(y).astype(BF16)
        for k in range(n_col // L):
            o_ref[c0 // L + k:c0 // L + k + 1, :] = y[:, k * L:(k + 1) * L].reshape(1, nc * L)


def _s5(u_t, ktab, m_in, m_out, a_pack, d_vec, bsz):
    chans, ntok = u_t.shape
    L = S5_CHUNK
    nc = ntok // L
    W = S5_GROUP_WIDTH
    wl = W * L
    return pl.pallas_call(
        functools.partial(_s5_kernel, bsz=bsz, n_col=512),
        grid=(chans // W,),
        in_specs=[pl.BlockSpec((W, ntok), lambda g: (g, 0)),
                  pl.BlockSpec((1,) + ktab.shape[1:], lambda g: (g, 0, 0, 0)),
                  pl.BlockSpec((1, wl, 2 * S5_STATE), lambda g: (g, 0, 0)),
                  pl.BlockSpec((1, 2 * S5_STATE, wl), lambda g: (g, 0, 0)),
                  pl.BlockSpec((1, 2, 2 * S5_STATE), lambda g: (g, 0, 0)),
                  pl.BlockSpec((1, 1, wl), lambda g: (g, 0, 0))],
        out_specs=pl.BlockSpec((W, ntok), lambda g: (g, 0)),
        out_shape=jax.ShapeDtypeStruct((chans, ntok), BF16),
        scratch_shapes=[pltpu.VMEM((wl, wl), BF16),
                        pltpu.VMEM((nc, 2 * S5_STATE), F32),
                        pltpu.VMEM((nc, 2 * S5_STATE), F32),
                        pltpu.VMEM((nc, 2 * S5_STATE), F32)],
        compiler_params=_cparams("parallel"),
        name="s5",
    )(u_t, ktab, m_in, m_out, a_pack, d_vec)


def _s5_tables(lam_re, lam_im, log_dt, b_re, b_im, c_re, c_im, d_skip):
    L = S5_CHUNK
    dt = jnp.exp(log_dt)[:, None]
    mag = jnp.exp(lam_re * dt)
    ang = lam_im * dt
    ar, ai = mag * jnp.cos(ang), mag * jnp.sin(ang)
    den = lam_re * lam_re + lam_im * lam_im
    cr = ((ar - 1.0) * lam_re + ai * lam_im) / den
    ci = (ai * lam_re - (ar - 1.0) * lam_im) / den
    bbr = cr[..., None] * b_re - ci[..., None] * b_im
    bbi = cr[..., None] * b_im + ci[..., None] * b_re
    pr, pi = jnp.ones_like(ar)[None], jnp.zeros_like(ar)[None]
    sr, si = ar, ai
    n = 1
    while n <= L:
        pr, pi = (jnp.concatenate([pr, pr * sr - pi * si], 0),
                  jnp.concatenate([pi, pr * si + pi * sr], 0))
        sr, si = sr * sr - si * si, 2.0 * sr * si
        n *= 2
    pr, pi = pr[:L + 1], pi[:L + 1]
    hp = lax.Precision.HIGHEST
    er = c_re[None] * pr[:L, :, None, :] - c_im[None] * pi[:L, :, None, :]
    ei = -(c_re[None] * pi[:L, :, None, :] + c_im[None] * pr[:L, :, None, :])
    ktab = (jnp.einsum('lgip,gpj->gjil', er, bbr, precision=hp)
            + jnp.einsum('lgip,gpj->gjil', ei, bbi, precision=hp))
    qr, qi = pr[:L][::-1], pi[:L][::-1]
    min_re = qr[:, :, :, None] * bbr[None] - qi[:, :, :, None] * bbi[None]
    min_im = qr[:, :, :, None] * bbi[None] + qi[:, :, :, None] * bbr[None]
    m_in = jnp.concatenate([min_re, min_im], axis=2)
    m_in = m_in.transpose(1, 3, 0, 2).reshape(m_in.shape[1], -1, 2 * S5_STATE)
    tr, ti = pr[1:L + 1], pi[1:L + 1]
    mo_re = c_re[None] * tr[:, :, None, :] - c_im[None] * ti[:, :, None, :]
    mo_im = -(c_re[None] * ti[:, :, None, :] + c_im[None] * tr[:, :, None, :])
    m_out = jnp.concatenate([mo_re, mo_im], axis=3)
    m_out = m_out.transpose(1, 3, 2, 0).reshape(m_out.shape[1], 2 * S5_STATE, -1)
    a_pack = jnp.stack([jnp.concatenate([pr[L], pr[L]], -1),
                        jnp.concatenate([-pi[L], pi[L]], -1)], axis=1)
    d_vec = jnp.repeat(d_skip, L, axis=-1)[:, None, :]
    groups, width = ktab.shape[:2]
    ktab = ktab.reshape(groups, width, width * L // LANES, LANES)
    return ktab, m_in.astype(BF16), m_out.astype(BF16), a_pack, d_vec


def _sgu_math(zu, zv, w_ref, bias_ref, lg_ref, lb_ref, o_ref):
    L = SGU_CHUNK
    row = lax.broadcasted_iota(jnp.int32, (L, L), 0)
    colm = lax.broadcasted_iota(jnp.int32, (L, L), 1)
    causal = colm <= row
    v = jax.nn.gelu(zv)
    mu = jnp.mean(v, axis=-1, keepdims=True)
    vc = v - mu
    var = jnp.mean(vc * vc, axis=-1, keepdims=True)
    vn = (vc * lax.rsqrt(var + EPS) * lg_ref[...] + lb_ref[...]).astype(BF16)
    gw = vn.shape[1] // SGU_GROUPS
    ws = [jnp.where(causal, w_ref[g], 0.0).astype(BF16) for g in range(SGU_GROUPS)]
    for c in range(vn.shape[0] // L):
        rows = slice(c * L, (c + 1) * L)
        mixed = jnp.concatenate(
            [_dot(ws[g], vn[rows, g * gw:(g + 1) * gw]) for g in range(SGU_GROUPS)], axis=-1)
        o_ref[0, rows, :] = (jax.nn.gelu(zu[rows]) * (mixed + bias_ref[...])).astype(BF16)


def _inproj_sgu_kernel(x_ref, sc_ref, sh_ref, wm_ref, wt_ref, w_ref, bias_ref, lg_ref, lb_ref,
                       o_ref, ot_ref):
    h = _norm_mod(x_ref[0], sc_ref[0], sh_ref[0]).astype(BF16)
    half = o_ref.shape[2]
    zv = _dot(h, wm_ref[:, half:])
    zu = _dot(h, wm_ref[:, :half])
    _sgu_math(zu, zv, w_ref, bias_ref, lg_ref, lb_ref, o_ref)
    ot_ref[...] = _dot_nt(wt_ref[...], h).astype(BF16)


def _inproj_sgu(x, sc, sh, w_sgu, w_t, w_s, bias_full, ln_gain, ln_bias, tm):
    bsz, s, d = x.shape
    half = w_t.shape[0]
    nt = s // tm
    return pl.pallas_call(
        _inproj_sgu_kernel,
        grid=(bsz, nt),
        in_specs=[pl.BlockSpec((1, tm, d), lambda b, i: (b, i, 0)),
                  pl.BlockSpec((1, 1, d), lambda b, i: (b, 0, 0)),
                  pl.BlockSpec((1, 1, d), lambda b, i: (b, 0, 0)),
                  _const_spec(w_sgu.shape), _const_spec(w_t.shape),
                  _const_spec(w_s.shape), _const_spec(bias_full.shape),
                  _const_spec((1, half)), _const_spec((1, half))],
        out_specs=[pl.BlockSpec((1, tm, half), lambda b, i: (b, i, 0)),
                   pl.BlockSpec((half, tm), lambda b, i: (0, b * nt + i))],
        out_shape=[jax.ShapeDtypeStruct((bsz, s, half), BF16),
                   jax.ShapeDtypeStruct((half, bsz * s), BF16)],
        compiler_params=_cparams("parallel", "parallel"),
        name="inproj_sgu",
    )(x, sc, sh, w_sgu, w_t, w_s, bias_full, ln_gain, ln_bias)


def _tail_kernel(x_ref, a_ref, b_ref, wo_ref, g1_ref, sc_ref, sh_ref, g2_ref, w1_ref, w2_ref,
                 *rest, f_chunk, with_glu):
    if with_glu:
        wg_ref, bg_ref, o_ref = rest
        af = a_ref[...].astype(F32).T
        a = af.astype(BF16)
        a = (af * jax.nn.sigmoid(_dot(a, wg_ref[...]) + bg_ref[...])).astype(BF16)
    else:
        (o_ref,) = rest
        a = a_ref[0]
    half = a.shape[1]
    y = _dot(a, wo_ref[:half, :]) + _dot(b_ref[0], wo_ref[half:, :])
    x1 = x_ref[0] + g1_ref[0] * y
    h = _norm_mod(x1, sc_ref[0], sh_ref[0]).astype(BF16)
    f = w1_ref.shape[1]
    acc = None
    for f0 in range(0, f, f_chunk):
        hid = jnp.maximum(_dot(h, w1_ref[:, f0:f0 + f_chunk]), 0.0)
        part = _dot((hid * hid).astype(BF16), w2_ref[f0:f0 + f_chunk, :])
        acc = part if acc is None else acc + part
    o_ref[0] = x1 + g2_ref[0] * acc


def _tail(x, mix_a, mix_b, w_out, g1, sc, sh, g2, w1, w2, glu, tm):
    bsz, s, d = x.shape
    half = mix_b.shape[-1]
    f = w1.shape[1]
    nt = s // tm
    row = pl.BlockSpec((1, 1, d), lambda b, i: (b, 0, 0))
    if glu is not None:
        a_spec = pl.BlockSpec((half, tm), lambda b, i: (0, b * nt + i))
    else:
        a_spec = pl.BlockSpec((1, tm, half), lambda b, i: (b, i, 0))
    in_specs = [pl.BlockSpec((1, tm, d), lambda b, i: (b, i, 0)),
                a_spec,
                pl.BlockSpec((1, tm, half), lambda b, i: (b, i, 0)),
                _const_spec((d, d)), row, row, row, row,
                _const_spec((d, f)), _const_spec((f, d))]
    args = [x, mix_a, mix_b, w_out, g1, sc, sh, g2, w1, w2]
    if glu is not None:
        in_specs += [_const_spec((half, half)), _const_spec((1, half))]
        args += list(glu)
    return pl.pallas_call(
        functools.partial(_tail_kernel, f_chunk=1024, with_glu=glu is not None),
        grid=(bsz, nt),
        in_specs=in_specs,
        out_specs=pl.BlockSpec((1, tm, d), lambda b, i: (b, i, 0)),
        out_shape=jax.ShapeDtypeStruct((bsz, s, d), F32),
        compiler_params=_cparams("parallel", "parallel"),
        name="tail",
    )(*args)


def _tile(s, want):
    t = min(s, want)
    assert s % t == 0, (s, t)
    return t


def kernel(x, c, ada_w, ada_b, even_w_in, even_w_out, gla_w_lr, gla_b_lr, gla_gain, fox_b_f, fox_q_gain, fox_k_gain, odd_w_in, odd_w_out, s5_lam_re, s5_lam_im, s5_log_dt, s5_b_re, s5_b_im, s5_c_re, s5_c_im, s5_d, s5_w_glu, s5_b_glu, sgu_ln_gain, sgu_ln_bias, sgu_w_s, sgu_b_s, mlp_w1, mlp_w2):
    bsz, s, d = x.shape
    half = d // 2
    n_heads = half // HEAD_DIM
    tm = _tile(s, 512)

    mod = _ada_mod(c, ada_w, ada_b)

    def mods(layer):
        return [m[:, None, :] for m in jnp.split(mod[layer], 6, axis=-1)]

    sh1, sc1, g1, sh2, sc2, g2 = mods(0)
    w_in = even_w_in[0]
    o_lr = 4 * half
    o_fq = o_lr + GLA_RANK
    o_ff = o_fq + 3 * half
    w_small = jnp.concatenate(
        [w_in[:, o_lr:o_fq], w_in[:, o_ff:],
         jnp.zeros((d, LANES - GLA_RANK - n_heads), w_in.dtype)], axis=1).astype(BF16)
    bf_pad = jnp.zeros((1, LANES), F32).at[0, FF_LANE0:FF_LANE0 + n_heads].set(fox_b_f[0])
    proj, small, qp, kp, vt, edges = _inproj0(
        x, sc1, sh1, w_in[:, :o_lr].astype(BF16), w_in[:, o_fq:o_fq + 2 * half].astype(BF16),
        w_in[:, o_fq + 2 * half:o_ff].T.astype(BF16), w_small, bf_pad,
        fox_q_gain[0].reshape(1, half), fox_k_gain[0].reshape(1, half), tm)

    wlr_pad = jnp.concatenate([gla_w_lr[0], jnp.zeros((LANES - GLA_RANK, half), F32)],
                              axis=0).astype(BF16)
    o_gla = _gla(proj, small, wlr_pad, gla_b_lr[0][None, :], gla_gain[0].reshape(1, half), tm)

    cq, ce, thr = _fox_skip_tables(edges, fox_q_gain[0], fox_k_gain[0], n_heads, tm // SUB)
    o_fox = _fox_attn(cq, ce, thr, qp, kp, vt, _tile(s, 2048))

    x = _tail(x, o_gla, o_fox, even_w_out[0].astype(BF16), g1, sc2, sh2, g2,
              mlp_w1[0].astype(BF16), mlp_w2[0].astype(BF16), None, tm)

    sh1, sc1, g1, sh2, sc2, g2 = mods(1)
    w_in = odd_w_in[0].astype(BF16)
    bias_full = jnp.repeat(sgu_b_s[0].T, half // SGU_GROUPS, axis=1)
    y_sgu, u_t = _inproj_sgu(x, sc1, sh1, w_in[:, half:], w_in[:, :half].T, sgu_w_s[0], bias_full,
                             sgu_ln_gain[0][None, :], sgu_ln_bias[0][None, :], tm)
    ktab, m_in, m_out, a_pack, d_vec = _s5_tables(
        s5_lam_re[0], s5_lam_im[0], s5_log_dt[0], s5_b_re[0], s5_b_im[0],
        s5_c_re[0], s5_c_im[0], s5_d[0])
    y_t = _s5(u_t, ktab, m_in, m_out, a_pack, d_vec, bsz)

    x = _tail(x, y_t, y_sgu, odd_w_out[0].astype(BF16), g1, sc2, sh2, g2,
              mlp_w1[1].astype(BF16), mlp_w2[1].astype(BF16),
              (s5_w_glu[0].astype(BF16), s5_b_glu[0][None, :]), tm)
    return x
```

```python
import functools

import numpy as np
import jax
import jax.numpy as jnp
from jax import lax
from jax.experimental import pallas as pl
from jax.experimental.pallas import tpu as pltpu

F32 = jnp.float32
BF16 = jnp.bfloat16

EPS = 1e-6
HEAD_DIM = 64
LANES = 128
GLA_RANK = 16
GLA_TAU = 16.0
GLA_CHUNK = 64
S5_GROUP_WIDTH = 16
S5_STATE = 64
S5_CHUNK = 128
SGU_GROUPS = 8
SGU_CHUNK = 128
VMEM_LIMIT_BYTES = 56 * 1024 * 1024
NEG_BIG = -1e30
LOG2E = 1.4426950408889634


def _cparams(*sem):
    return pltpu.CompilerParams(dimension_semantics=sem, vmem_limit_bytes=VMEM_LIMIT_BYTES)


def _const_spec(shape):
    return pl.BlockSpec(shape, lambda *_: (0,) * len(shape), pipeline_mode=pl.Buffered(1))


def _dot(a, b):
    return jnp.dot(a, b, preferred_element_type=F32)


def _dot_nt(a, b):
    return lax.dot_general(a, b, (((1,), (1,)), ((), ())), preferred_element_type=F32)


def _dot_tn(a, b):
    return lax.dot_general(a, b, (((0,), (0,)), ((), ())), preferred_element_type=F32)


def _split3(x):
    hi = x.astype(BF16)
    r = x - hi.astype(F32)
    mid = r.astype(BF16)
    lo = (r - mid.astype(F32)).astype(BF16)
    return hi, mid, lo


def _log_sigmoid(z):
    return jnp.minimum(z, 0.0) - jnp.log(1.0 + jnp.exp(-jnp.abs(z)))


def _head_mean_matrix():
    return jnp.asarray(np.kron(np.eye(LANES // HEAD_DIM, dtype=np.float32),
                               np.full((HEAD_DIM, HEAD_DIM), 1.0 / HEAD_DIM, np.float32)), BF16)


def _norm_mod(x, sc, sh):
    ms = jnp.mean(x * x, axis=-1, keepdims=True)
    return x * lax.rsqrt(ms + EPS) * (1.0 + sc) + sh


def _ada_kernel(c_ref, w_ref, b_ref, o_ref):
    c = c_ref[...]
    ca = c * jax.nn.sigmoid(c)
    o_ref[0] = _dot(ca, w_ref[0]) + b_ref[0]


def _ada_mod(c, ada_w, ada_b):
    depth, d, n = ada_w.shape
    bsz = c.shape[0]
    tn = n // 4
    return pl.pallas_call(
        _ada_kernel,
        grid=(depth, n // tn),
        in_specs=[pl.BlockSpec((bsz, d), lambda l, j: (0, 0)),
                  pl.BlockSpec((1, d, tn), lambda l, j: (l, 0, j)),
                  pl.BlockSpec((1, 1, tn), lambda l, j: (l, 0, j))],
        out_specs=pl.BlockSpec((1, bsz, tn), lambda l, j: (l, 0, j)),
        out_shape=jax.ShapeDtypeStruct((depth, bsz, n), F32),
        compiler_params=_cparams("parallel", "parallel"),
        name="ada_mod",
    )(c, ada_w, ada_b.reshape(depth, 1, n))


def _gla_kernel(q_ref, k_ref, v_ref, g_ref, sm_ref, wlr_ref, blr_ref, gain_ref, tril_ref, bd_ref,
                o_ref, state_ref, a_ref, kv_ref, st_ref, qin_ref, ebl_ref, o_acc_ref,
                *, n_heads, n_chunks):
    C = GLA_CHUNK
    mid = C // 2 - 1

    @pl.when(pl.program_id(1) == 0)
    def _():
        state_ref[...] = jnp.zeros_like(state_ref)

    row = lax.broadcasted_iota(jnp.int32, (C, C), 0)
    col = lax.broadcasted_iota(jnp.int32, (C, C), 1)
    causal = col <= row
    z = _dot(sm_ref[0].astype(BF16), wlr_ref[...]) + blr_ref[...]
    lc = (_log_sigmoid(z) * (1.0 / GLA_TAU)).astype(BF16)
    bc_all = _dot(tril_ref[...], lc)
    heads = [slice(h * HEAD_DIM, (h + 1) * HEAD_DIM) for h in range(n_heads)]

    for c in range(n_chunks):
        rows = slice(c * C, (c + 1) * C)
        bc = bc_all[rows]
        m = bc[mid:mid + 1, :]
        bl = bc[C - 1:C, :]
        e = jnp.exp(bc - m)
        qt = q_ref[0, rows, :].astype(F32) * (HEAD_DIM ** -0.5) * e
        kt = k_ref[0, rows, :].astype(F32) * (1.0 / e)
        qin_ref[rows, :] = (qt * jnp.exp(m)).astype(BF16)
        kst = (kt * jnp.exp(bl - m)).astype(BF16)
        qt = qt.astype(BF16)
        kt = kt.astype(BF16)
        ebl_ref[c:c + 1, :] = jnp.exp(bl)
        v = v_ref[0, rows, :]
        for h, sl in enumerate(heads):
            a_ref[c, h] = jnp.where(causal, _dot_nt(qt[:, sl], kt[:, sl]), 0.0).astype(BF16)
            kv_ref[c, h] = _dot_tn(v[:, sl], kst[:, sl])

    for h, sl in enumerate(heads):
        st = state_ref[h]
        for c in range(n_chunks):
            st_ref[c, h] = st.astype(BF16)
            st = st * ebl_ref[c:c + 1, sl] + kv_ref[c, h]
        state_ref[h] = st

    for c in range(n_chunks):
        rows = slice(c * C, (c + 1) * C)
        outs = [_dot(a_ref[c, h], v_ref[0, rows, sl]) + _dot_nt(qin_ref[rows, sl], st_ref[c, h])
                for h, sl in enumerate(heads)]
        o_acc_ref[rows, :] = jnp.concatenate(outs, axis=-1)

    bd = bd_ref[...]
    for p in range(n_heads * HEAD_DIM // LANES):
        cols = slice(p * LANES, (p + 1) * LANES)
        o = o_acc_ref[:, cols]
        ms = _dot((o * o).astype(BF16), bd)
        g = g_ref[0, :, cols].astype(F32)
        o_ref[0, :, cols] = (o * lax.rsqrt(ms + EPS) * gain_ref[:, cols]
                             * (g * jax.nn.sigmoid(g))).astype(BF16)


def _gla(proj, small, wlr_pad, blr, gain, tt):
    bsz, s, _ = proj.shape
    w = wlr_pad.shape[1]
    n_heads = w // HEAD_DIM

    def col(j):
        return pl.BlockSpec((1, tt, w), lambda b, i, j=j: (b, i, j))

    n_chunks = tt // GLA_CHUNK
    tril = jnp.asarray(np.kron(np.eye(n_chunks, dtype=np.float32),
                               np.tril(np.ones((GLA_CHUNK, GLA_CHUNK), np.float32))), BF16)
    bd = _head_mean_matrix()
    return pl.pallas_call(
        functools.partial(_gla_kernel, n_heads=n_heads, n_chunks=n_chunks),
        grid=(bsz, s // tt),
        in_specs=[col(0), col(1), col(2), col(3),
                  pl.BlockSpec((1, tt, LANES), lambda b, i: (b, i, 0)),
                  _const_spec((LANES, w)), _const_spec((1, w)), _const_spec((1, w)),
                  _const_spec((tt, tt)), _const_spec((LANES, LANES))],
        out_specs=pl.BlockSpec((1, tt, w), lambda b, i: (b, i, 0)),
        out_shape=jax.ShapeDtypeStruct((bsz, s, w), BF16),
        scratch_shapes=[pltpu.VMEM((n_heads, HEAD_DIM, HEAD_DIM), F32),
                        pltpu.VMEM((n_chunks, n_heads, GLA_CHUNK, GLA_CHUNK), BF16),
                        pltpu.VMEM((n_chunks, n_heads, HEAD_DIM, HEAD_DIM), F32),
                        pltpu.VMEM((n_chunks, n_heads, HEAD_DIM, HEAD_DIM), BF16),
                        pltpu.VMEM((tt, w), BF16),
                        pltpu.VMEM((n_chunks, w), F32),
                        pltpu.VMEM((tt, w), F32)],
        compiler_params=_cparams("parallel", "arbitrary"),
        name="gla",
    )(proj, proj, proj, proj, small, wlr_pad, blr, gain, tril, bd)


FF_LANE0 = GLA_RANK
C_LANE0 = HEAD_DIM
X_LANE0 = 96
SUB = 256
V_ROWS = HEAD_DIM + 16


def _fox_prep_math(fq, fk, vt, sm, bf_ref, qg_ref, kg_ref, tril_ref, bd_ref, sel_ref, cq_ref, ind_ref,
                   qo_ref, ko_ref, vo_ref, co_ref, carry_ref, n_heads):
    tt = sm.shape[0]
    lane = lax.broadcasted_iota(jnp.int32, (tt, LANES), 1)
    is_ff = (lane >= FF_LANE0) & (lane < FF_LANE0 + n_heads)
    ls = jnp.where(is_ff, _log_sigmoid(sm + bf_ref[...]), 0.0)
    tril = tril_ref[...]
    p0, p1, p2 = _split3(ls)
    cum = carry_ref[0:1, :] + (_dot(tril, p0) + _dot(tril, p1) + _dot(tril, p2))
    carry_ref[0:1, :] = cum[tt - 1:tt, :]
    edge_rows = []
    for jb in range(tt // SUB):
        edge_rows += [cum[jb * SUB:jb * SUB + 1, :], cum[(jb + 1) * SUB - 1:(jb + 1) * SUB, :]]
    edge_rows.append(jnp.zeros((co_ref.shape[2] - len(edge_rows), LANES), F32))
    co_ref[0, 0] = jnp.concatenate(edge_rows, axis=0)
    c0, c1, c2 = _split3(cum * LOG2E)
    e = _dot(c0, sel_ref[0]) + _dot(c1, sel_ref[1]) + _dot(c2, sel_ref[2])
    ex_q = e[:, :LANES] + cq_ref[...]
    ek_all = e[:, LANES:]
    bd = bd_ref[...]
    low = lane < HEAD_DIM
    is_x = (lane >= X_LANE0) & (lane < X_LANE0 + 3)

    def normed(x_all, gain_ref, scale, pair):
        x = x_all[:, pair * LANES:(pair + 1) * LANES]
        ms = _dot((x * x).astype(BF16), bd)
        return x * lax.rsqrt(ms + EPS) * (gain_ref[:, pair * LANES:(pair + 1) * LANES] * scale)

    for pair in range(n_heads // 2):
        qn = normed(fq, qg_ref, HEAD_DIM ** -0.5 * LOG2E, pair)
        kn = normed(fk, kg_ref, 1.0, pair)
        for odd in range(2):
            h = 2 * pair + odd
            if odd:
                qh = pltpu.roll(qn, HEAD_DIM, axis=1)
                kh = pltpu.roll(kn, HEAD_DIM, axis=1)
            else:
                qh, kh = qn, kn
            ex_k = jnp.where(is_x, pltpu.roll(ek_all, (X_LANE0 - 16 * h) % LANES, axis=1),
                             ind_ref[h:h + 1, :])
            qo_ref[0, h] = jnp.where(low, qh, ex_q).astype(BF16)
            ko_ref[0, h] = jnp.where(low, kh, ex_k).astype(BF16)

    ones_rows = (lax.broadcasted_iota(jnp.int32, (V_ROWS - HEAD_DIM, SUB), 0) == 0).astype(BF16)
    for h in range(n_heads):
        for jb in range(tt // SUB):
            vo_ref[0, h, jb, :HEAD_DIM, :] = vt[h * HEAD_DIM:(h + 1) * HEAD_DIM,
                                                jb * SUB:(jb + 1) * SUB].astype(BF16)
            vo_ref[0, h, jb, HEAD_DIM:, :] = ones_rows


def _inproj0_kernel(x_ref, sc_ref, sh_ref, wg_ref, wf_ref, wvt_ref, ws_ref, bf_ref, qg_ref, kg_ref, tril_ref,
                    bd_ref, sel_ref, cq_ref, ind_ref, og_ref, os_ref, qo_ref, ko_ref, vo_ref, co_ref,
                    carry_ref, *, n_heads):
    @pl.when(pl.program_id(1) == 0)
    def _():
        carry_ref[...] = jnp.zeros_like(carry_ref)

    h = _norm_mod(x_ref[0], sc_ref[0], sh_ref[0]).astype(BF16)
    w = og_ref.shape[2] // 4
    sm = _dot(h, ws_ref[...])
    os_ref[0] = sm
    fq = _dot(h, wf_ref[:, :w])
    fk = _dot(h, wf_ref[:, w:])
    vt = _dot_nt(wvt_ref[...], h)
    _fox_prep_math(fq, fk, vt, sm, bf_ref, qg_ref, kg_ref, tril_ref, bd_ref, sel_ref, cq_ref, ind_ref,
                   qo_ref, ko_ref, vo_ref, co_ref, carry_ref, n_heads)
    for c0 in range(0, 4 * w, w):
        og_ref[0, :, c0:c0 + w] = _dot(h, wg_ref[:, c0:c0 + w]).astype(BF16)


def _fox_prep_consts(n_heads):
    sel = np.zeros((3, LANES, 2 * LANES), np.float32)
    cq = np.zeros((1, LANES), np.float32)
    ind = np.zeros((n_heads, LANES), np.float32)
    for h in range(n_heads):
        for r in range(3):
            sel[r, FF_LANE0 + h, C_LANE0 + 3 * h + r] = 1.0
            sel[r, FF_LANE0 + h, LANES + 16 * h + r] = -1.0
            ind[h, C_LANE0 + 3 * h + r] = 1.0
            cq[0, X_LANE0 + r] = 1.0
    return jnp.asarray(sel, BF16), jnp.asarray(cq), jnp.asarray(ind), _head_mean_matrix()


def _inproj0(x, sc, sh, w_gla, w_fox, w_vt, w_small, bf_pad, q_gain, k_gain, tm):
    bsz, s, d = x.shape
    w = q_gain.shape[1]
    n_heads = w // HEAD_DIM
    sel, cq, ind, bd = _fox_prep_consts(n_heads)
    tril = jnp.asarray(np.tril(np.ones((tm, tm), np.float32)), BF16)
    head_spec = pl.BlockSpec((1, n_heads, tm, LANES), lambda b, i: (b, 0, i, 0))
    head_sds = jax.ShapeDtypeStruct((bsz, n_heads, s, LANES), BF16)
    return pl.pallas_call(
        functools.partial(_inproj0_kernel, n_heads=n_heads),
        grid=(bsz, s // tm),
        in_specs=[pl.BlockSpec((1, tm, d), lambda b, i: (b, i, 0)),
                  pl.BlockSpec((1, 1, d), lambda b, i: (b, 0, 0)),
                  pl.BlockSpec((1, 1, d), lambda b, i: (b, 0, 0)),
                  _const_spec(w_gla.shape), _const_spec(w_fox.shape), _const_spec(w_vt.shape),
                  _const_spec(w_small.shape),
                  _const_spec((1, LANES)), _const_spec((1, w)), _const_spec((1, w)),
                  _const_spec((tm, tm)), _const_spec((LANES, LANES)),
                  _const_spec(sel.shape), _const_spec(cq.shape), _const_spec(ind.shape)],
        out_specs=[pl.BlockSpec((1, tm, 4 * w), lambda b, i: (b, i, 0)),
                   pl.BlockSpec((1, tm, LANES), lambda b, i: (b, i, 0)),
                   head_spec, head_spec,
                   pl.BlockSpec((1, n_heads, tm // SUB, V_ROWS, SUB), lambda b, i: (b, 0, i, 0, 0)),
                   pl.BlockSpec((1, 1, 8, LANES), lambda b, i: (b, i, 0, 0))],
        out_shape=[jax.ShapeDtypeStruct((bsz, s, 4 * w), BF16),
                   jax.ShapeDtypeStruct((bsz, s, LANES), F32),
                   head_sds, head_sds,
                   jax.ShapeDtypeStruct((bsz, n_heads, s // SUB, V_ROWS, SUB), BF16),
                   jax.ShapeDtypeStruct((bsz, s // tm, 8, LANES), F32)],
        scratch_shapes=[pltpu.VMEM((8, LANES), F32)],
        compiler_params=_cparams("parallel", "arbitrary"),
        name="inproj0",
    )(x, sc, sh, w_gla, w_fox, w_vt, w_small, bf_pad, q_gain, k_gain, tril, bd, sel, cq, ind)


ZERO_MARGIN = 160.0
LOOKAHEAD = 4


def _fox_attn_kernel(cq_ref, ce_ref, thr_ref, q_ref, k_ref, vt_ref, o_ref, m_ref, acc_ref, s_ref,
                     tri_ref, *, n_sub):
    b = pl.program_id(0)
    pair = pl.program_id(1)
    i = pl.program_id(2)
    row = lax.broadcasted_iota(jnp.int32, (SUB, SUB), 0)
    col = lax.broadcasted_iota(jnp.int32, (SUB, SUB), 1)
    tri_ref[...] = jnp.where(row <= col, 0.0, NEG_BIG)

    def blocks(r):
        a = i * n_sub + r
        return a, jnp.maximum(a - 1, 0)

    chains = [(r, hh) for r in range(n_sub) for hh in range(2)]

    def scores(r, hh):
        a, jp = blocks(r)
        q = q_ref[0, hh, r * SUB:(r + 1) * SUB, :]
        s_ref[r, hh, :SUB, :] = _dot_nt(k_ref[0, hh, pl.ds(pl.multiple_of(jp * SUB, SUB), SUB), :], q)
        s_ref[r, hh, SUB:, :] = _dot_nt(k_ref[0, hh, pl.ds(pl.multiple_of(a * SUB, SUB), SUB), :], q)

    def softmax_pv(r, hh):
        a, jp = blocks(r)
        no_prev = jnp.where(a == 0, NEG_BIG, 0.0)
        st = s_ref[r, hh, :SUB, :] + no_prev
        sd = s_ref[r, hh, SUB:, :] + tri_ref[...]
        m = jnp.maximum(jnp.max(st, axis=0, keepdims=True), jnp.max(sd, axis=0, keepdims=True))
        pt = jnp.exp2((st - m).astype(BF16))
        pd = jnp.exp2((sd - m).astype(BF16))
        acc_ref[r, hh] = _dot(vt_ref[0, hh, jp], pt) + _dot(vt_ref[0, hh, a], pd)
        m_ref[r, hh, 0:1, :] = m

    for n in range(len(chains) + LOOKAHEAD):
        if n < len(chains):
            scores(*chains[n])
        if n >= LOOKAHEAD:
            softmax_pv(*chains[n - LOOKAHEAD])

    def reaches(r, hh, j):
        h = 2 * pair + hh
        a = i * n_sub + r
        return (j >= 0) & (cq_ref[b, h, a] - ce_ref[b, h, jnp.maximum(j, 0)] >= thr_ref[h])

    any_more = functools.reduce(jnp.logical_or,
                                [reaches(r, hh, i * n_sub + r - 2) for r, hh in chains])

    @pl.when(any_more)
    def _():
        for r, hh in chains:
            q = q_ref[0, hh, r * SUB:(r + 1) * SUB, :]

            def body(c, hh=hh, q=q):
                j, m, acc = c
                kb = k_ref[0, hh, pl.ds(pl.multiple_of(j * SUB, SUB), SUB), :]
                s = _dot_nt(kb, q)
                m_new = jnp.maximum(m, jnp.max(s, axis=0, keepdims=True))
                p = jnp.exp2((s - m_new).astype(BF16))
                return j - 1, m_new, jnp.exp2(m - m_new) * acc + _dot(vt_ref[0, hh, j], p)

            _, m, acc = lax.while_loop(lambda c, r=r, hh=hh: reaches(r, hh, c[0]), body,
                                       (i * n_sub + r - 2, m_ref[r, hh, 0:1, :], acc_ref[r, hh]))
            acc_ref[r, hh] = acc
            m_ref[r, hh, 0:1, :] = m

    for r in range(n_sub):
        outs = [acc_ref[r, hh, :HEAD_DIM, :] / acc_ref[r, hh, HEAD_DIM:HEAD_DIM + 1, :]
                for hh in range(2)]
        o_ref[0, r * SUB:(r + 1) * SUB, :] = jnp.concatenate(outs, axis=0).T.astype(BF16)


def _fox_attn(cq, ce, thr, qp, kp, vt, tq):
    bsz, n_heads, s, _ = qp.shape
    nkb = s // SUB
    n_sub = tq // SUB
    grid_spec = pltpu.PrefetchScalarGridSpec(
        num_scalar_prefetch=3,
        grid=(bsz, n_heads // 2, s // tq),
        in_specs=[pl.BlockSpec((1, 2, tq, LANES), lambda b, p, i, *_: (b, p, i, 0)),
                  pl.BlockSpec((1, 2, s, LANES), lambda b, p, i, *_: (b, p, 0, 0)),
                  pl.BlockSpec((1, 2, nkb, V_ROWS, SUB), lambda b, p, i, *_: (b, p, 0, 0, 0))],
        out_specs=pl.BlockSpec((1, tq, LANES), lambda b, p, i, *_: (b, i, p)),
        scratch_shapes=[pltpu.VMEM((n_sub, 2, 8, SUB), F32),
                        pltpu.VMEM((n_sub, 2, V_ROWS, SUB), F32),
                        pltpu.VMEM((n_sub, 2, 2 * SUB, SUB), F32),
                        pltpu.VMEM((SUB, SUB), F32)])
    return pl.pallas_call(
        functools.partial(_fox_attn_kernel, n_sub=n_sub),
        grid_spec=grid_spec,
        out_shape=jax.ShapeDtypeStruct((bsz, s, n_heads * HEAD_DIM), BF16),
        compiler_params=_cparams("parallel", "parallel", "arbitrary"),
        name="fox_attn",
    )(cq, ce, thr, qp, kp, vt)


def _fox_skip_tables(edges, q_gain, k_gain, n_heads, blocks):
    bsz, nt = edges.shape[:2]
    c2 = edges[:, :, :2 * blocks, FF_LANE0:FF_LANE0 + n_heads] * LOG2E
    c2 = c2.reshape(bsz, nt * blocks, 2, n_heads)
    cq = c2[:, :, 0, :].transpose(0, 2, 1)
    ce = c2[:, :, 1, :].transpose(0, 2, 1)
    bound = (jnp.max(jnp.abs(q_gain), axis=-1) * jnp.max(jnp.abs(k_gain), axis=-1)
             * (HEAD_DIM ** 0.5) * LOG2E)
    thr = -(2.0 * 1.02 * bound + ZERO_MARGIN)
    return cq, ce, thr


def _s5_kernel(u_ref, kt_ref, min_ref, mout_ref, a_ref, d_ref, o_ref,
               t_ref, v_ref, vs_ref, xp_ref, *, bsz, n_col):
    L = S5_CHUNK
    W = S5_GROUP_WIDTH
    per = LANES // L
    nc = u_ref.shape[1] // L
    row = lax.broadcasted_iota(jnp.int32, (L, LANES), 0)
    lane = lax.broadcasted_iota(jnp.int32, (L, LANES), 1)
    causal = (lane % L) >= row

    tiles_per_chunk = n_col // LANES

    def build(chunk):
        for j in range(W):
            for ip in range(chunk * tiles_per_chunk, (chunk + 1) * tiles_per_chunk):
                tile = jnp.broadcast_to(kt_ref[0, j, ip:ip + 1, :], (L, LANES))
                tile = pltpu.roll(tile, 0, axis=1, stride=1, stride_axis=0)
                t_ref[j * L:(j + 1) * L, ip * LANES:(ip + 1) * LANES] = (
                    jnp.where(causal, tile, 0.0).astype(BF16))

    build(0)

    u = jnp.concatenate([u_ref[j:j + 1, :].reshape(nc, L) for j in range(W)], axis=-1)
    v = _dot(u, min_ref[0])
    v_ref[...] = v
    vs_ref[...] = pltpu.roll(v, S5_STATE, axis=1)
    a1 = a_ref[0, 0:1, :]
    a2 = a_ref[0, 1:2, :]

    n_steps = nc // bsz

    def scan(c, carry):
        x, xs = carry
        rows = pl.ds(c, bsz, stride=n_steps)
        xp_ref[rows, :] = x
        return (x * a1 + xs * a2 + v_ref[rows, :], xs * a1 - x * a2 + vs_ref[rows, :])

    zero = jnp.zeros((bsz, 2 * S5_STATE), F32)
    lax.fori_loop(0, n_steps, scan, (zero, zero), unroll=8)

    xp = xp_ref[...].astype(BF16)
    for c0 in range(0, W * L, n_col):
        cs = slice(c0, c0 + n_col)
        if c0 + n_col < W * L:
            build(c0 // n_col + 1)
        y = (_dot(u, t_ref[:, cs]) + _dot(xp, mout_ref[0, :, cs])
             + d_ref[0, :, cs] * u[:, cs].astype(F32))
        y = jax.nn.gelu(y).astype(BF16)
        for k in range(n_col // L):
            o_ref[c0 // L + k:c0 // L + k + 1, :] = y[:, k * L:(k + 1) * L].reshape(1, nc * L)


def _s5(u_t, ktab, m_in, m_out, a_pack, d_vec, bsz):
    chans, ntok = u_t.shape
    L = S5_CHUNK
    nc = ntok // L
    W = S5_GROUP_WIDTH
    wl = W * L
    return pl.pallas_call(
        functools.partial(_s5_kernel, bsz=bsz, n_col=512),
        grid=(chans // W,),
        in_specs=[pl.BlockSpec((W, ntok), lambda g: (g, 0)),
                  pl.BlockSpec((1,) + ktab.shape[1:], lambda g: (g, 0, 0, 0)),
                  pl.BlockSpec((1, wl, 2 * S5_STATE), lambda g: (g, 0, 0)),
                  pl.BlockSpec((1, 2 * S5_STATE, wl), lambda g: (g, 0, 0)),
                  pl.BlockSpec((1, 2, 2 * S5_STATE), lambda g: (g, 0, 0)),
                  pl.BlockSpec((1, 1, wl), lambda g: (g, 0, 0))],
        out_specs=pl.BlockSpec((W, ntok), lambda g: (g, 0)),
        out_shape=jax.ShapeDtypeStruct((chans, ntok), BF16),
        scratch_shapes=[pltpu.VMEM((wl, wl), BF16),
                        pltpu.VMEM((nc, 2 * S5_STATE), F32),
                        pltpu.VMEM((nc, 2 * S5_STATE), F32),
                        pltpu.VMEM((nc, 2 * S5_STATE), F32)],
        compiler_params=_cparams("parallel"),
        name="s5",
    )(u_t, ktab, m_in, m_out, a_pack, d_vec)


def _s5_tables(lam_re, lam_im, log_dt, b_re, b_im, c_re, c_im, d_skip):
    L = S5_CHUNK
    dt = jnp.exp(log_dt)[:, None]
    mag = jnp.exp(lam_re * dt)
    ang = lam_im * dt
    ar, ai = mag * jnp.cos(ang), mag * jnp.sin(ang)
    den = lam_re * lam_re + lam_im * lam_im
    cr = ((ar - 1.0) * lam_re + ai * lam_im) / den
    ci = (ai * lam_re - (ar - 1.0) * lam_im) / den
    bbr = cr[..., None] * b_re - ci[..., None] * b_im
    bbi = cr[..., None] * b_im + ci[..., None] * b_re
    pr, pi = jnp.ones_like(ar)[None], jnp.zeros_like(ar)[None]
    sr, si = ar, ai
    n = 1
    while n <= L:
        pr, pi = (jnp.concatenate([pr, pr * sr - pi * si], 0),
                  jnp.concatenate([pi, pr * si + pi * sr], 0))
        sr, si = sr * sr - si * si, 2.0 * sr * si
        n *= 2
    pr, pi = pr[:L + 1], pi[:L + 1]
    hp = lax.Precision.HIGHEST
    er = c_re[None] * pr[:L, :, None, :] - c_im[None] * pi[:L, :, None, :]
    ei = -(c_re[None] * pi[:L, :, None, :] + c_im[None] * pr[:L, :, None, :])
    ktab = (jnp.einsum('lgip,gpj->gjil', er, bbr, precision=hp)
            + jnp.einsum('lgip,gpj->gjil', ei, bbi, precision=hp))
    qr, qi = pr[:L][::-1], pi[:L][::-1]
    min_re = qr[:, :, :, None] * bbr[None] - qi[:, :, :, None] * bbi[None]
    min_im = qr[:, :, :, None] * bbi[None] + qi[:, :, :, None] * bbr[None]
    m_in = jnp.concatenate([min_re, min_im], axis=2)
    m_in = m_in.transpose(1, 3, 0, 2).reshape(m_in.shape[1], -1, 2 * S5_STATE)
    tr, ti = pr[1:L + 1], pi[1:L + 1]
    mo_re = c_re[None] * tr[:, :, None, :] - c_im[None] * ti[:, :, None, :]
    mo_im = -(c_re[None] * ti[:, :, None, :] + c_im[None] * tr[:, :, None, :])
    m_out = jnp.concatenate([mo_re, mo_im], axis=3)
    m_out = m_out.transpose(1, 3, 2, 0).reshape(m_out.shape[1], 2 * S5_STATE, -1)
    a_pack = jnp.stack([jnp.concatenate([pr[L], pr[L]], -1),
                        jnp.concatenate([-pi[L], pi[L]], -1)], axis=1)
    d_vec = jnp.repeat(d_skip, L, axis=-1)[:, None, :]
    groups, width = ktab.shape[:2]
    ktab = ktab.reshape(groups, width, width * L // LANES, LANES)
    return ktab, m_in.astype(BF16), m_out.astype(BF16), a_pack, d_vec


def _sgu_math(zu, zv, w_ref, bias_ref, lg_ref, lb_ref, o_ref):
    L = SGU_CHUNK
    row = lax.broadcasted_iota(jnp.int32, (L, L), 0)
    colm = lax.broadcasted_iota(jnp.int32, (L, L), 1)
    causal = colm <= row
    v = jax.nn.gelu(zv)
    mu = jnp.mean(v, axis=-1, keepdims=True)
    vc = v - mu
    var = jnp.mean(vc * vc, axis=-1, keepdims=True)
    vn = (vc * lax.rsqrt(var + EPS) * lg_ref[...] + lb_ref[...]).astype(BF16)
    gw = vn.shape[1] // SGU_GROUPS
    ws = [jnp.where(causal, w_ref[g], 0.0).astype(BF16) for g in range(SGU_GROUPS)]
    for c in range(vn.shape[0] // L):
        rows = slice(c * L, (c + 1) * L)
        mixed = jnp.concatenate(
            [_dot(ws[g], vn[rows, g * gw:(g + 1) * gw]) for g in range(SGU_GROUPS)], axis=-1)
        o_ref[0, rows, :] = (jax.nn.gelu(zu[rows]) * (mixed + bias_ref[...])).astype(BF16)


def _inproj_sgu_kernel(x_ref, sc_ref, sh_ref, wm_ref, wt_ref, w_ref, bias_ref, lg_ref, lb_ref,
                       o_ref, ot_ref):
    h = _norm_mod(x_ref[0], sc_ref[0], sh_ref[0]).astype(BF16)
    half = o_ref.shape[2]
    zv = _dot(h, wm_ref[:, half:])
    zu = _dot(h, wm_ref[:, :half])
    _sgu_math(zu, zv, w_ref, bias_ref, lg_ref, lb_ref, o_ref)
    ot_ref[...] = _dot_nt(wt_ref[...], h).astype(BF16)


def _inproj_sgu(x, sc, sh, w_sgu, w_t, w_s, bias_full, ln_gain, ln_bias, tm):
    bsz, s, d = x.shape
    half = w_t.shape[0]
    nt = s // tm
    return pl.pallas_call(
        _inproj_sgu_kernel,
        grid=(bsz, nt),
        in_specs=[pl.BlockSpec((1, tm, d), lambda b, i: (b, i, 0)),
                  pl.BlockSpec((1, 1, d), lambda b, i: (b, 0, 0)),
                  pl.BlockSpec((1, 1, d), lambda b, i: (b, 0, 0)),
                  _const_spec(w_sgu.shape), _const_spec(w_t.shape),
                  _const_spec(w_s.shape), _const_spec(bias_full.shape),
                  _const_spec((1, half)), _const_spec((1, half))],
        out_specs=[pl.BlockSpec((1, tm, half), lambda b, i: (b, i, 0)),
                   pl.BlockSpec((half, tm), lambda b, i: (0, b * nt + i))],
        out_shape=[jax.ShapeDtypeStruct((bsz, s, half), BF16),
                   jax.ShapeDtypeStruct((half, bsz * s), BF16)],
        compiler_params=_cparams("parallel", "parallel"),
        name="inproj_sgu",
    )(x, sc, sh, w_sgu, w_t, w_s, bias_full, ln_gain, ln_bias)


def _tail_kernel(x_ref, a_ref, b_ref, wo_ref, g1_ref, sc_ref, sh_ref, g2_ref, w1_ref, w2_ref,
                 *rest, f_chunk, with_glu):
    if with_glu:
        wg_ref, bg_ref, o_ref = rest
        af = a_ref[...].astype(F32).T
        a = af.astype(BF16)
        a = (af * jax.nn.sigmoid(_dot(a, wg_ref[...]) + bg_ref[...])).astype(BF16)
    else:
        (o_ref,) = rest
        a = a_ref[0]
    half = a.shape[1]
    y = _dot(a, wo_ref[:half, :]) + _dot(b_ref[0], wo_ref[half:, :])
    x1 = x_ref[0] + g1_ref[0] * y
    h = _norm_mod(x1, sc_ref[0], sh_ref[0]).astype(BF16)
    f = w1_ref.shape[1]
    acc = None
    for f0 in range(0, f, f_chunk):
        hid = jnp.maximum(_dot(h, w1_ref[:, f0:f0 + f_chunk]), 0.0)
        part = _dot((hid * hid).astype(BF16), w2_ref[f0:f0 + f_chunk, :])
        acc = part if acc is None else acc + part
    o_ref[0] = x1 + g2_ref[0] * acc


def _tail(x, mix_a, mix_b, w_out, g1, sc, sh, g2, w1, w2, glu, tm):
    bsz, s, d = x.shape
    half = mix_b.shape[-1]
    f = w1.shape[1]
    nt = s // tm
    row = pl.BlockSpec((1, 1, d), lambda b, i: (b, 0, 0))
    if glu is not None:
        a_spec = pl.BlockSpec((half, tm), lambda b, i: (0, b * nt + i))
    else:
        a_spec = pl.BlockSpec((1, tm, half), lambda b, i: (b, i, 0))
    in_specs = [pl.BlockSpec((1, tm, d), lambda b, i: (b, i, 0)),
                a_spec,
                pl.BlockSpec((1, tm, half), lambda b, i: (b, i, 0)),
                _const_spec((d, d)), row, row, row, row,
                _const_spec((d, f)), _const_spec((f, d))]
    args = [x, mix_a, mix_b, w_out, g1, sc, sh, g2, w1, w2]
    if glu is not None:
        in_specs += [_const_spec((half, half)), _const_spec((1, half))]
        args += list(glu)
    return pl.pallas_call(
        functools.partial(_tail_kernel, f_chunk=1024, with_glu=glu is not None),
        grid=(bsz, nt),
        in_specs=in_specs,
        out_specs=pl.BlockSpec((1, tm, d), lambda b, i: (b, i, 0)),
        out_shape=jax.ShapeDtypeStruct((bsz, s, d), F32),
        compiler_params=_cparams("parallel", "parallel"),
        name="tail",
    )(*args)


def _tile(s, want):
    t = min(s, want)
    assert s % t == 0, (s, t)
    return t


def kernel(x, c, ada_w, ada_b, even_w_in, even_w_out, gla_w_lr, gla_b_lr, gla_gain, fox_b_f, fox_q_gain, fox_k_gain, odd_w_in, odd_w_out, s5_lam_re, s5_lam_im, s5_log_dt, s5_b_re, s5_b_im, s5_c_re, s5_c_im, s5_d, s5_w_glu, s5_b_glu, sgu_ln_gain, sgu_ln_bias, sgu_w_s, sgu_b_s, mlp_w1, mlp_w2):
    bsz, s, d = x.shape
    half = d // 2
    n_heads = half // HEAD_DIM
    tm = _tile(s, 512)

    mod = _ada_mod(c, ada_w, ada_b)

    def mods(layer):
        return [m[:, None, :] for m in jnp.split(mod[layer], 6, axis=-1)]

    sh1, sc1, g1, sh2, sc2, g2 = mods(0)
    w_in = even_w_in[0]
    o_lr = 4 * half
    o_fq = o_lr + GLA_RANK
    o_ff = o_fq + 3 * half
    w_small = jnp.concatenate(
        [w_in[:, o_lr:o_fq], w_in[:, o_ff:],
         jnp.zeros((d, LANES - GLA_RANK - n_heads), w_in.dtype)], axis=1).astype(BF16)
    bf_pad = jnp.zeros((1, LANES), F32).at[0, FF_LANE0:FF_LANE0 + n_heads].set(fox_b_f[0])
    proj, small, qp, kp, vt, edges = _inproj0(
        x, sc1, sh1, w_in[:, :o_lr].astype(BF16), w_in[:, o_fq:o_fq + 2 * half].astype(BF16),
        w_in[:, o_fq + 2 * half:o_ff].T.astype(BF16), w_small, bf_pad,
        fox_q_gain[0].reshape(1, half), fox_k_gain[0].reshape(1, half), tm)

    wlr_pad = jnp.concatenate([gla_w_lr[0], jnp.zeros((LANES - GLA_RANK, half), F32)],
                              axis=0).astype(BF16)
    o_gla = _gla(proj, small, wlr_pad, gla_b_lr[0][None, :], gla_gain[0].reshape(1, half), tm)

    cq, ce, thr = _fox_skip_tables(edges, fox_q_gain[0], fox_k_gain[0], n_heads, tm // SUB)
    o_fox = _fox_attn(cq, ce, thr, qp, kp, vt, _tile(s, 2048))

    x = _tail(x, o_gla, o_fox, even_w_out[0].astype(BF16), g1, sc2, sh2, g2,
              mlp_w1[0].astype(BF16), mlp_w2[0].astype(BF16), None, tm)

    sh1, sc1, g1, sh2, sc2, g2 = mods(1)
    w_in = odd_w_in[0].astype(BF16)
    bias_full = jnp.repeat(sgu_b_s[0].T, half // SGU_GROUPS, axis=1)
    y_sgu, u_t = _inproj_sgu(x, sc1, sh1, w_in[:, half:], w_in[:, :half].T, sgu_w_s[0], bias_full,
                             sgu_ln_gain[0][None, :], sgu_ln_bias[0][None, :], tm)
    ktab, m_in, m_out, a_pack, d_vec = _s5_tables(
        s5_lam_re[0], s5_lam_im[0], s5_log_dt[0], s5_b_re[0], s5_b_im[0],
        s5_c_re[0], s5_c_im[0], s5_d[0])
    y_t = _s5(u_t, ktab, m_in, m_out, a_pack, d_vec, bsz)

    x = _tail(x, y_t, y_sgu, odd_w_out[0].astype(BF16), g1, sc2, sh2, g2,
              mlp_w1[1].astype(BF16), mlp_w2[1].astype(BF16),
              (s5_w_glu[0].astype(BF16), s5_b_glu[0][None, :]), tm)
    return x
```
